```python
import math
import jax
import jax.numpy as jnp
from jax import lax
import numpy as np


D_MODEL = 1024
BATCH = 4
SEQ = 4096
DEPTH = 2

GRID_W = 64
CTX_LEN = 256
N_MIXERS = 2

DEEPNORM_ALPHA = (2 * DEPTH) ** 0.25
DEEPNORM_BETA = (8 * DEPTH) ** -0.25
LN_EPS = 1e-5
RMS_EPS = 1e-6
GATED_NORM_EPS = 1e-5

SSD_D_INNER = 2 * D_MODEL
SSD_HEAD_DIM = 64
SSD_HEADS = SSD_D_INNER // SSD_HEAD_DIM
SSD_GROUPS = 8
SSD_STATE = 128
SSD_CONV = 5
SSD_CHUNK = 128
SSD_CONV_DIM = SSD_D_INNER + 2 * SSD_GROUPS * SSD_STATE
SSD_IN_DIM = SSD_D_INNER + SSD_CONV_DIM + 2 * SSD_HEADS
N_SSD_LAYERS = (DEPTH + 1) // 2

MLA_HEADS = 8
MLA_NOPE = 128
MLA_ROPE = 64
MLA_V = 128
MLA_Q_LORA = 384
MLA_KV_LORA = 256
MLA_IN_DIM = MLA_Q_LORA + MLA_KV_LORA + MLA_ROPE
MLA_SCALE = (MLA_NOPE + MLA_ROPE) ** -0.5
ROPE_FREQS = MLA_ROPE // 4
ROPE_BASE = 10000.0
Q_BLOCK = 128
N_MLA_LAYERS = DEPTH // 2

N_EXPERTS = 32
TOP_K = 4
D_FF = D_MODEL
SWIGLU_LIMIT = 7.0
SWIGLU_ALPHA = 1.702
MOE_BLOCK = 128

kernel_name = 'hybrid_ssd_mla_moe_flow_block'


def layer_norm(x, g, b):
    xf = x.astype(jnp.float32)
    mu = jnp.mean(xf, axis=-1, keepdims=True)
    var = jnp.mean(jnp.square(xf - mu), axis=-1, keepdims=True)
    return ((xf - mu) * lax.rsqrt(var + LN_EPS) * g + b).astype(x.dtype)


def rms_norm(x, g):
    xf = x.astype(jnp.float32)
    return (xf * lax.rsqrt(jnp.mean(xf * xf, axis=-1, keepdims=True) + RMS_EPS) * g).astype(x.dtype)


def centred_depthwise_conv(u, w, b):
    pad = SSD_CONV // 2
    y = lax.conv_general_dilated(u, w[:, None, :], (1,), [(pad, pad)],
                                 dimension_numbers=('NWC', 'WIO', 'NWC'),
                                 feature_group_count=u.shape[-1])
    return y + b


def ssd_chunked(xh, dt, a, bm, cm, s0):
    f32 = jnp.float32
    bsz, L, H, P = xh.shape
    G, N = bm.shape[2], bm.shape[3]
    R = H // G
    nc = L // SSD_CHUNK
    x = (xh.astype(f32) * dt[..., None]).reshape(bsz, nc, SSD_CHUNK, G, R, P)
    la = (dt * a).reshape(bsz, nc, SSD_CHUNK, G, R)
    bc = bm.astype(f32).reshape(bsz, nc, SSD_CHUNK, G, N)
    cc = cm.astype(f32).reshape(bsz, nc, SSD_CHUNK, G, N)
    cum = jnp.cumsum(la, axis=2)
    lower = jnp.tril(jnp.ones((SSD_CHUNK, SSD_CHUNK), bool))[:, :, None, None]
    seg = cum[:, :, :, None] - cum[:, :, None, :]
    decay = jnp.where(lower, jnp.exp(jnp.where(lower, seg, 0.0)), 0.0)
    cb = jnp.einsum('bclgn,bcsgn->bclsg', cc, bc)
    y_diag = jnp.einsum('bclsg,bclsgr,bcsgrp->bclgrp', cb, decay, x)
    to_end = jnp.exp(cum[:, :, -1:] - cum)
    chunk_states = jnp.einsum('bclgn,bclgr,bclgrp->bcgrpn', bc, to_end, x)
    chunk_decay = jnp.exp(cum[:, :, -1])

    def carry_state(s, inp):
        cs, cd = inp
        return s * cd[..., None, None] + cs, s

    s_final, s_in = lax.scan(carry_state, s0.astype(f32).reshape(bsz, G, R, P, N),
                             (jnp.moveaxis(chunk_states, 1, 0), jnp.moveaxis(chunk_decay, 1, 0)))
    s_in = jnp.moveaxis(s_in, 0, 1)
    y_off = jnp.einsum('bclgn,bcgrpn,bclgr->bclgrp', cc, s_in, jnp.exp(cum))
    y = (y_diag + y_off).reshape(bsz, L, H, P)
    return y, s_final.reshape(bsz, H, P, N)


def ssd_mixer(h_lat, h_ctx, w_in, conv_w, conv_b, dt_bias, a_log, d_skip, norm_w, w_out, need_ctx):
    f32 = jnp.float32
    a = -jnp.exp(a_log.astype(f32))

    def branch(h):
        bsz, L = h.shape[:2]
        u = h @ w_in
        z, xbc, dt_raw = jnp.split(u, [SSD_D_INNER, SSD_D_INNER + SSD_CONV_DIM], axis=-1)
        xbc = jax.nn.silu(centred_depthwise_conv(xbc, conv_w, conv_b))
        xs, bm, cm = jnp.split(xbc, [SSD_D_INNER, SSD_D_INNER + SSD_GROUPS * SSD_STATE], axis=-1)
        xh = xs.reshape(bsz, L, SSD_HEADS, SSD_HEAD_DIM)
        bm = bm.reshape(bsz, L, SSD_GROUPS, SSD_STATE)
        cm = cm.reshape(bsz, L, SSD_GROUPS, SSD_STATE)
        dt = jax.nn.softplus(dt_raw.astype(f32).reshape(bsz, L, 2, SSD_HEADS) + dt_bias)
        return z, xh, bm, cm, dt

    def flip(t):
        return jnp.flip(t, axis=1)

    zc, xc, bc, cc, dtc = branch(h_ctx)
    zl, xl, bl, cl, dtl = branch(h_lat)
    bsz = h_lat.shape[0]
    s0 = jnp.zeros((bsz, SSD_HEADS, SSD_HEAD_DIM, SSD_STATE), f32)
    yc_f, sc_f = ssd_chunked(xc, dtc[:, :, 0], a[0], bc, cc, s0)
    yl_f, _ = ssd_chunked(xl, dtl[:, :, 0], a[0], bl, cl, sc_f)
    yc_b, sc_b = ssd_chunked(flip(xc), flip(dtc[:, :, 1]), a[1], flip(bc), flip(cc), s0)
    yl_b, _ = ssd_chunked(flip(xl), flip(dtl[:, :, 1]), a[1], flip(bl), flip(cl), sc_b)

    def finish(xh, y_fwd, y_bwd_rev, z):
        bsz_, L = xh.shape[:2]
        y = y_fwd + flip(y_bwd_rev) + xh.astype(f32) * d_skip[:, None]
        y = y.reshape(bsz_, L, SSD_D_INNER) * jax.nn.silu(z.astype(f32))
        yg = y.reshape(bsz_, L, SSD_GROUPS, SSD_D_INNER // SSD_GROUPS)
        yg = yg * lax.rsqrt(jnp.mean(yg * yg, axis=-1, keepdims=True) + GATED_NORM_EPS)
        y = yg.reshape(bsz_, L, SSD_D_INNER) * norm_w
        return y.astype(w_out.dtype) @ w_out

    y_lat = finish(xl, yl_f, yl_b, zl)
    y_ctx = finish(xc, yc_f, yc_b, zc) if need_ctx else None
    return y_lat, y_ctx


def axial_rope_tables(seq_len):
    t = jnp.arange(seq_len)
    row = (t // GRID_W).astype(jnp.float32)
    col = (t % GRID_W).astype(jnp.float32)
    inv = ROPE_BASE ** (-jnp.arange(ROPE_FREQS, dtype=jnp.float32) / ROPE_FREQS)
    ang = jnp.stack([row[:, None] * inv, col[:, None] * inv], axis=1)
    return jnp.cos(ang), jnp.sin(ang)


def apply_rope_2d(t, cos, sin):
    tr = t.reshape(t.shape[:-1] + (2, 2, ROPE_FREQS)).astype(jnp.float32)
    t1, t2 = tr[..., 0, :], tr[..., 1, :]
    out = jnp.stack([t1 * cos - t2 * sin, t2 * cos + t1 * sin], axis=-2)
    return out.reshape(t.shape).astype(t.dtype)


def mla_attend(qn, qr, kn, kr, v):
    s = (jnp.einsum('bqhd,bkhd->bhqk', qn, kn, preferred_element_type=jnp.float32)
         + jnp.einsum('bqhr,bkr->bhqk', qr, kr, preferred_element_type=jnp.float32))
    p = jax.nn.softmax(s * MLA_SCALE, axis=-1)
    return jnp.einsum('bhqk,bkhv->bqhv', p.astype(v.dtype), v)


def mla_blocked(qn, qr, kn, kr, v):
    bsz, L = qn.shape[:2]
    nb = L // Q_BLOCK

    def to_blocks(t):
        return jnp.moveaxis(t.reshape((bsz, nb, Q_BLOCK) + t.shape[2:]), 1, 0)

    out = lax.map(lambda qs: mla_attend(qs[0], qs[1], kn, kr, v), (to_blocks(qn), to_blocks(qr)))
    return jnp.moveaxis(out, 0, 1).reshape(bsz, L, MLA_HEADS, MLA_V)


def mla_mixer(h_lat, h_ctx, w_a, q_norm, kv_norm, w_qb, w_kvb, w_o, rope_cos, rope_sin, need_ctx):
    def project(h):
        bsz, L = h.shape[:2]
        q_lat, kv_lat, k_rope = jnp.split(h @ w_a, [MLA_Q_LORA, MLA_Q_LORA + MLA_KV_LORA], axis=-1)
        kv = (rms_norm(kv_lat, kv_norm) @ w_kvb).reshape(bsz, L, MLA_HEADS, MLA_NOPE + MLA_V)
        return q_lat, kv[..., :MLA_NOPE], kv[..., MLA_NOPE:], k_rope

    def queries(q_lat):
        bsz, L = q_lat.shape[:2]
        q = (rms_norm(q_lat, q_norm) @ w_qb).reshape(bsz, L, MLA_HEADS, MLA_NOPE + MLA_ROPE)
        return q[..., :MLA_NOPE], q[..., MLA_NOPE:]

    bsz, L = h_lat.shape[:2]
    ql_lat, kn_l, v_l, kr_l = project(h_lat)
    qn_l, qr_l = queries(ql_lat)
    qr_l = apply_rope_2d(qr_l, rope_cos[:, None], rope_sin[:, None])
    kr_l = apply_rope_2d(kr_l, rope_cos, rope_sin)
    ql_ctx, kn_c, v_c, kr_c = project(h_ctx)
    kn = jnp.concatenate([kn_c, kn_l], axis=1)
    kr = jnp.concatenate([kr_c, kr_l], axis=1)
    v = jnp.concatenate([v_c, v_l], axis=1)
    o_lat = mla_blocked(qn_l, qr_l, kn, kr, v)
    y_lat = o_lat.reshape(bsz, L, MLA_HEADS * MLA_V) @ w_o
    y_ctx = None
    if need_ctx:
        qn_c, qr_c = queries(ql_ctx)
        o_ctx = mla_attend(qn_c, qr_c, kn_c, kr_c, v_c)
        y_ctx = o_ctx.reshape(bsz, h_ctx.shape[1], MLA_HEADS * MLA_V) @ w_o
    return y_lat, y_ctx


def clamped_swiglu(u):
    glu, lin = jnp.split(u, 2, axis=-1)
    glu = jnp.minimum(glu, SWIGLU_LIMIT)
    lin = jnp.clip(lin, -SWIGLU_LIMIT, SWIGLU_LIMIT)
    return glu * jax.nn.sigmoid(SWIGLU_ALPHA * glu) * (lin + 1.0)


def moe_ffn(h, router_w, router_b, w_in, b_in, w_out, b_out):
    T, D = h.shape
    logits = jnp.dot(h, router_w, preferred_element_type=jnp.float32) + router_b
    top_logit, top_e = lax.top_k(logits, TOP_K)
    gate = jax.nn.softmax(top_logit, axis=-1)
    n_assign = T * TOP_K
    flat_e = top_e.reshape(-1)
    order = jnp.argsort(flat_e, stable=True)
    sorted_e = flat_e[order]
    sorted_tok = order // TOP_K
    sorted_gate = gate.reshape(-1)[order]
    counts = jnp.bincount(flat_e, length=N_EXPERTS)
    padded = (counts + MOE_BLOCK - 1) // MOE_BLOCK * MOE_BLOCK
    padded_end = jnp.cumsum(padded)
    group_start = jnp.cumsum(counts) - counts
    slot = (padded_end - padded)[sorted_e] + jnp.arange(n_assign) - group_start[sorted_e]
    n_blocks = (n_assign + N_EXPERTS * (MOE_BLOCK - 1) + MOE_BLOCK - 1) // MOE_BLOCK
    n_slots = n_blocks * MOE_BLOCK
    slot_tok = jnp.zeros((n_slots,), jnp.int32).at[slot].set(sorted_tok)
    block_e = jnp.minimum(jnp.searchsorted(padded_end, jnp.arange(n_blocks) * MOE_BLOCK, side='right'),
                          N_EXPERTS - 1)
    xb = h[slot_tok].reshape(n_blocks, MOE_BLOCK, D)

    def expert_block(args):
        xblk, e = args
        u = xblk @ w_in[e] + b_in[e]
        return clamped_swiglu(u) @ w_out[e] + b_out[e]

    yb = lax.map(expert_block, (xb, block_e)).reshape(n_slots, D)
    contrib = yb[slot].astype(jnp.float32) * sorted_gate[:, None]
    return jax.ops.segment_sum(contrib, sorted_tok, num_segments=T).astype(h.dtype)


def setup_inputs(seed: int = 0) -> dict:
    key = jax.random.key(seed)
    keys = iter(jax.random.split(key, 48))
    f32 = jnp.float32

    def normal(shape, std):
        return jax.random.normal(next(keys), shape, f32) * std

    def dense(shape, fan_in, gain=1.0):
        return normal(shape, gain * fan_in ** -0.5)

    def near_one(shape):
        return 1.0 + normal(shape, 0.02)

    def small(shape):
        return normal(shape, 0.02)

    dt0 = jnp.exp(jax.random.uniform(next(keys), (N_SSD_LAYERS, 2, SSD_HEADS), f32)
                  * (math.log(0.1) - math.log(0.001)) + math.log(0.001))
    dt_bias = dt0 + jnp.log(-jnp.expm1(-dt0))
    a_log = jnp.log(jax.random.uniform(next(keys), (N_SSD_LAYERS, 2, SSD_HEADS), f32, 1.0, 16.0))
    return {
        'x': normal((BATCH, SEQ, D_MODEL), 1.0),
        'c': normal((BATCH, D_MODEL), 1.0),
        'ctx': normal((BATCH, CTX_LEN, D_MODEL), 1.0),
        'c_ctx': normal((D_MODEL,), 1.0),
        'mod_w': dense((DEPTH, D_MODEL, 6 * D_MODEL), D_MODEL),
        'mod_b': small((DEPTH, 6 * D_MODEL)),
        'ln1_g': near_one((DEPTH, D_MODEL)),
        'ln1_b': small((DEPTH, D_MODEL)),
        'ln2_g': near_one((DEPTH, D_MODEL)),
        'ln2_b': small((DEPTH, D_MODEL)),
        'ssd_w_in': dense((N_SSD_LAYERS, D_MODEL, SSD_IN_DIM), D_MODEL),
        'ssd_conv_w': dense((N_SSD_LAYERS, SSD_CONV, SSD_CONV_DIM), SSD_CONV),
        'ssd_conv_b': small((N_SSD_LAYERS, SSD_CONV_DIM)),
        'ssd_dt_bias': dt_bias,
        'ssd_a_log': a_log,
        'ssd_d': near_one((N_SSD_LAYERS, SSD_HEADS)),
        'ssd_norm_w': near_one((N_SSD_LAYERS, SSD_D_INNER)),
        'ssd_w_out': dense((N_SSD_LAYERS, SSD_D_INNER, D_MODEL), SSD_D_INNER, DEEPNORM_BETA),
        'mla_w_a': dense((N_MLA_LAYERS, D_MODEL, MLA_IN_DIM), D_MODEL),
        'mla_q_norm': near_one((N_MLA_LAYERS, MLA_Q_LORA)),
        'mla_kv_norm': near_one((N_MLA_LAYERS, MLA_KV_LORA)),
        'mla_w_qb': dense((N_MLA_LAYERS, MLA_Q_LORA, MLA_HEADS * (MLA_NOPE + MLA_ROPE)), MLA_Q_LORA),
        'mla_w_kvb': dense((N_MLA_LAYERS, MLA_KV_LORA, MLA_HEADS * (MLA_NOPE + MLA_V)), MLA_KV_LORA),
        'mla_w_o': dense((N_MLA_LAYERS, MLA_HEADS * MLA_V, D_MODEL), MLA_HEADS * MLA_V, DEEPNORM_BETA),
        'router_w': dense((DEPTH, D_MODEL, N_EXPERTS), D_MODEL),
        'router_b': normal((DEPTH, N_EXPERTS), 0.01),
        'moe_w_in': dense((DEPTH, N_EXPERTS, D_MODEL, 2 * D_FF), D_MODEL),
        'moe_b_in': small((DEPTH, N_EXPERTS, 2 * D_FF)),
        'moe_w_out': dense((DEPTH, N_EXPERTS, D_FF, D_MODEL), D_FF, DEEPNORM_BETA),
        'moe_b_out': small((DEPTH, N_EXPERTS, D_MODEL)),
    }


def reference(x, c, ctx, c_ctx, mod_w, mod_b, ln1_g, ln1_b, ln2_g, ln2_b,
              ssd_w_in, ssd_conv_w, ssd_conv_b, ssd_dt_bias, ssd_a_log, ssd_d, ssd_norm_w, ssd_w_out,
              mla_w_a, mla_q_norm, mla_kv_norm, mla_w_qb, mla_w_kvb, mla_w_o,
              router_w, router_b, moe_w_in, moe_b_in, moe_w_out, moe_b_out):
    bsz, seq_len, d = x.shape
    n_lat = bsz * seq_len
    rope_cos, rope_sin = axial_rope_tables(seq_len)
    for i in range(DEPTH):
        last = i == DEPTH - 1
        j = i // N_MIXERS
        mod_l = jax.nn.silu(c) @ mod_w[i] + mod_b[i]
        mod_c = jax.nn.silu(c_ctx) @ mod_w[i] + mod_b[i]
        sh1, sc1, g1, sh2, sc2, g2 = jnp.split(mod_l[:, None, :], 6, axis=-1)
        csh1, csc1, cg1, csh2, csc2, cg2 = jnp.split(mod_c, 6, axis=-1)
        h_lat = x * (1.0 + sc1) + sh1
        h_ctx = ctx * (1.0 + csc1) + csh1
        if i % N_MIXERS == 0:
            y_lat, y_ctx = ssd_mixer(h_lat, h_ctx, ssd_w_in[j], ssd_conv_w[j], ssd_conv_b[j],
                                     ssd_dt_bias[j], ssd_a_log[j], ssd_d[j], ssd_norm_w[j],
                                     ssd_w_out[j], not last)
        else:
            y_lat, y_ctx = mla_mixer(h_lat, h_ctx, mla_w_a[j], mla_q_norm[j], mla_kv_norm[j],
                                     mla_w_qb[j], mla_w_kvb[j], mla_w_o[j], rope_cos, rope_sin,
                                     not last)
        x = layer_norm(DEEPNORM_ALPHA * x + g1 * y_lat, ln1_g[i], ln1_b[i])
        tokens = (x * (1.0 + sc2) + sh2).reshape(n_lat, d)
        if not last:
            ctx = layer_norm(DEEPNORM_ALPHA * ctx + cg1 * y_ctx, ln1_g[i], ln1_b[i])
            tokens = jnp.concatenate([tokens, (ctx * (1.0 + csc2) + csh2).reshape(-1, d)], axis=0)
        f = moe_ffn(tokens, router_w[i], router_b[i], moe_w_in[i], moe_b_in[i], moe_w_out[i], moe_b_out[i])
        x = layer_norm(DEEPNORM_ALPHA * x + g2 * f[:n_lat].reshape(x.shape), ln2_g[i], ln2_b[i])
        if not last:
            ctx = layer_norm(DEEPNORM_ALPHA * ctx + cg2 * f[n_lat:].reshape(ctx.shape), ln2_g[i], ln2_b[i])
    return x
```

```python
import functools
import math

import jax
import jax.numpy as jnp
from jax import lax
from jax.experimental import pallas as pl
from jax.experimental.pallas import tpu as pltpu

F32, BF16, I32 = jnp.float32, jnp.bfloat16, jnp.int32
HIGHEST = lax.Precision.HIGHEST

DEPTH = 2
GRID_W = 64
DEEPNORM_ALPHA = (2 * DEPTH) ** 0.25
LN_EPS = 1e-5
RMS_EPS = 1e-6
GATED_NORM_EPS = 1e-5
SSD_HEAD_DIM = 64
SSD_GROUPS = 8
SSD_STATE = 128
SSD_CONV = 5
SSD_CHUNK = 128
MLA_HEADS = 8
MLA_NOPE = 128
MLA_ROPE = 64
MLA_V = 128
MLA_Q_LORA = 384
MLA_KV_LORA = 256
MLA_SCALE = (MLA_NOPE + MLA_ROPE) ** -0.5
ROPE_FREQS = MLA_ROPE // 4
ROPE_BASE = 10000.0
N_EXPERTS = 32
TOP_K = 4
SWIGLU_LIMIT = 7.0
SWIGLU_ALPHA = 1.702

LANES = 128
ROW_TILE = 256
MOE_TILE = 256
ROUTE_TILE = 512
ATTN_Q_TILE = 256
ATTN_K_CHUNK = 512
HALO = 16
MOD_ROWS = 8
VMEM_LIMIT = 48 * 1024 * 1024
VMEM_LIMIT_MOE = 56 * 1024 * 1024


def _cparams(n_axes, vmem=VMEM_LIMIT):
    return pltpu.CompilerParams(dimension_semantics=("arbitrary",) * n_axes, vmem_limit_bytes=vmem)


def _sigmoid(v):
    return 1.0 / (1.0 + jnp.exp(-v))


def _softplus(v):
    return jnp.maximum(v, 0.0) + jnp.log1p(jnp.exp(-jnp.abs(v)))


def _layer_norm(v, g, b):
    mu = jnp.mean(v, axis=-1, keepdims=True)
    d = v - mu
    var = jnp.mean(d * d, axis=-1, keepdims=True)
    return d * lax.rsqrt(var + LN_EPS) * g + b


def _rms_norm(v, g):
    return v * lax.rsqrt(jnp.mean(v * v, axis=-1, keepdims=True) + RMS_EPS) * g


def _dot(a, b):
    return jnp.dot(a, b, preferred_element_type=F32)


def _dot_nt(a, b):
    return lax.dot_general(a, b, (((1,), (1,)), ((), ())), preferred_element_type=F32)


def _dot_tn(a, b):
    return lax.dot_general(a, b, (((0,), (0,)), ((), ())), preferred_element_type=F32)


def _split_bf16(v):
    hi = v.astype(BF16)
    lo = (v - hi.astype(F32)).astype(BF16)
    return hi, lo


def _mod_kernel(c_ref, w_ref, b_ref, o_ref):
    c = c_ref[...]
    s = (c * _sigmoid(c)).astype(BF16)
    o_ref[0] = _dot(s, w_ref[0].astype(BF16)) + b_ref[0]


def adaln_mods(cond, mod_w, mod_b):
    depth, d, n = mod_w.shape
    tn = n // 4
    return pl.pallas_call(
        _mod_kernel,
        grid=(depth, n // tn),
        in_specs=[pl.BlockSpec((MOD_ROWS, d), lambda l, j: (0, 0)),
                  pl.BlockSpec((1, d, tn), lambda l, j: (l, 0, j)),
                  pl.BlockSpec((1, 1, tn), lambda l, j: (l, 0, j))],
        out_specs=pl.BlockSpec((1, MOD_ROWS, tn), lambda l, j: (l, 0, j)),
        out_shape=jax.ShapeDtypeStruct((depth, MOD_ROWS, n), F32),
        compiler_params=_cparams(2),
        name="adaln_mods",
    )(cond, mod_w, mod_b.reshape(depth, 1, n))


def _mk_mod_spec(d, which, row_fn):
    return pl.BlockSpec((1, 1, d), lambda *idx: (row_fn(*idx) * 6 + which, 0, 0))


def _ssd_in_kernel(x_ref, sc_ref, sh_ref, w_ref, wdt_ref, wdtT_ref, bias_ref, biasT_ref,
                   u_ref, dt_ref, dtT_ref, *, n_heads):
    h = (x_ref[0] * (1.0 + sc_ref[0]) + sh_ref[0]).astype(BF16)
    n_out = u_ref.shape[2]
    step = 512
    for n in range(n_out // step):
        u_ref[0, :, n * step:(n + 1) * step] = _dot(h, w_ref[:, n * step:(n + 1) * step]).astype(BF16)
    dt = _softplus(_dot(h, wdt_ref[...]) + bias_ref[...])
    dt_ref[0, 0] = dt[:, 0:n_heads]
    dt_ref[0, 1] = dt[:, n_heads:2 * n_heads]
    dtT = _softplus(_dot_nt(wdtT_ref[...], h) + biasT_ref[...])
    dtT_ref[0, 0] = dtT[0:n_heads]
    dtT_ref[0, 1] = dtT[n_heads:2 * n_heads]


def ssd_in_proj(xs, mods, w_zx, w_dt, w_dtT, dt_bias, dt_biasT, *, ctx_tiles, ctx_row, n_heads):
    bsz, s, d = xs.shape
    n_out = w_zx.shape[1]
    tm = ROW_TILE
    row = lambda b, j: jnp.where(j < ctx_tiles, ctx_row, b)
    return pl.pallas_call(
        functools.partial(_ssd_in_kernel, n_heads=n_heads),
        grid=(bsz, s // tm),
        in_specs=[pl.BlockSpec((1, tm, d), lambda b, j: (b, j, 0)),
                  _mk_mod_spec(d, 1, row), _mk_mod_spec(d, 0, row),
                  pl.BlockSpec((d, n_out), lambda b, j: (0, 0)),
                  pl.BlockSpec((d, LANES), lambda b, j: (0, 0)),
                  pl.BlockSpec((2 * n_heads, d), lambda b, j: (0, 0)),
                  pl.BlockSpec((1, LANES), lambda b, j: (0, 0)),
                  pl.BlockSpec((2 * n_heads, 1), lambda b, j: (0, 0))],
        out_specs=[pl.BlockSpec((1, tm, n_out), lambda b, j: (b, j, 0)),
                   pl.BlockSpec((1, 2, tm, n_heads), lambda b, j: (b, 0, j, 0)),
                   pl.BlockSpec((1, 2, n_heads, tm), lambda b, j: (b, 0, 0, j))],
        out_shape=[jax.ShapeDtypeStruct((bsz, s, n_out), BF16),
                   jax.ShapeDtypeStruct((bsz, 2, s, n_heads), F32),
                   jax.ShapeDtypeStruct((bsz, 2, n_heads, s), F32)],
        compiler_params=_cparams(2),
        name="ssd_in_proj",
    )(xs, mods, mods, w_zx, w_dt, w_dtT, dt_bias, dt_biasT)


def _ssd_conv_kernel(prev_ref, cur_ref, next_ref, w_ref, b_ref, o_ref, scr, *, tc, ctx_tiles, n_tiles):
    j = pl.program_id(1)
    has_prev = jnp.logical_and(j != 0, j != ctx_tiles)
    has_next = jnp.logical_and(j != ctx_tiles - 1, j != n_tiles - 1)
    scr[0:HALO] = jnp.where(has_prev, prev_ref[0].astype(F32), 0.0)
    scr[HALO:HALO + tc] = cur_ref[0].astype(F32)
    scr[HALO + tc:2 * HALO + tc] = jnp.where(has_next, next_ref[0].astype(F32), 0.0)
    pad = SSD_CONV // 2
    acc = b_ref[...] + w_ref[0:1] * scr[HALO - pad:HALO - pad + tc]
    for k in range(1, SSD_CONV):
        acc = acc + w_ref[k:k + 1] * scr[HALO - pad + k:HALO - pad + k + tc]
    o_ref[0] = (acc * _sigmoid(acc)).astype(BF16)


def ssd_conv(u, conv_w, conv_b, *, col0, ctx_tiles):
    bsz, s, _ = u.shape
    c = conv_w.shape[1]
    tc, wc = ROW_TILE, 2048
    n_tiles = s // tc
    cb0 = col0 // wc
    hb = tc // HALO
    return pl.pallas_call(
        functools.partial(_ssd_conv_kernel, tc=tc, ctx_tiles=ctx_tiles, n_tiles=n_tiles),
        grid=(bsz, n_tiles, c // wc),
        in_specs=[pl.BlockSpec((1, HALO, wc), lambda b, j, k: (b, jnp.maximum(j * hb - 1, 0), cb0 + k)),
                  pl.BlockSpec((1, tc, wc), lambda b, j, k: (b, j, cb0 + k)),
                  pl.BlockSpec((1, HALO, wc), lambda b, j, k: (b, jnp.minimum((j + 1) * hb, s // HALO - 1), cb0 + k)),
                  pl.BlockSpec((8, wc), lambda b, j, k: (0, k)),
                  pl.BlockSpec((1, wc), lambda b, j, k: (0, k))],
        out_specs=pl.BlockSpec((1, tc, wc), lambda b, j, k: (b, j, k)),
        out_shape=jax.ShapeDtypeStruct((bsz, s, c), BF16),
        scratch_shapes=[pltpu.VMEM((tc + 2 * HALO, wc), F32)],
        compiler_params=_cparams(3),
        name="ssd_conv",
    )(u, u, u, conv_w, conv_b)


def _ssd_scan_kernel(x_ref, b_ref, c_ref, dt_ref, dtT_ref, alog_ref, alogT_ref, tri_ref, triT_ref, e_ref,
                     y_ref, st_ref, *, groups, heads_per_group, head_dim, n_state):
    step = pl.program_id(2)

    @pl.when(step == 0)
    def _():
        st_ref[...] = jnp.zeros_like(st_ref)

    L = x_ref.shape[1]
    n_heads = groups * heads_per_group
    gw = heads_per_group * head_dim
    dt = dt_ref[0, 0]
    dtT = dtT_ref[0, 0]
    la = dt * (-jnp.exp(alog_ref[0]))
    laT = dtT * (-jnp.exp(alogT_ref[0]))
    tri = tri_ref[0]
    cum = jnp.dot(tri, la, precision=HIGHEST, preferred_element_type=F32)
    cumT = jnp.dot(laT, triT_ref[0], precision=HIGHEST, preferred_element_type=F32)
    tot = jnp.sum(la, axis=0, keepdims=True)
    ecum = jnp.exp(cum)
    w_end = dt * jnp.exp(tot - cum)
    cdec = jnp.exp(tot)
    stacked = jnp.concatenate([w_end, ecum, jnp.broadcast_to(cdec, (8, n_heads))], axis=0)
    s_hi, s_lo = _split_bf16(stacked)
    expanded = _dot(s_hi, e_ref[...]) + _dot(s_lo, e_ref[...])
    w_x, ecum_x, cdec_x = expanded[0:L], expanded[L:2 * L], expanded[2 * L:2 * L + 1]
    mask = tri > 0.5
    lane = lax.broadcasted_iota(I32, (L, 2 * head_dim), 1)
    for g in range(groups):
        bg = b_ref[0, :, g * n_state:(g + 1) * n_state]
        cg = c_ref[0, :, g * n_state:(g + 1) * n_state]
        cb = _dot_nt(cg, bg)
        mats = []
        for r in range(heads_per_group):
            h = g * heads_per_group + r
            seg = cum[:, h:h + 1] - cumT[h:h + 1, :]
            dec = jnp.exp(jnp.where(mask, seg, 0.0))
            mats.append((jnp.where(mask, cb * dec, 0.0) * dtT[h:h + 1, :]).astype(BF16))
        pairs = []
        for q in range(heads_per_group // 2):
            c0 = g * gw + q * 2 * head_dim
            xp = x_ref[0, :, c0:c0 + 2 * head_dim]
            pairs.append(jnp.where(lane < head_dim, _dot(mats[2 * q], xp), _dot(mats[2 * q + 1], xp)))
        y_diag = jnp.concatenate(pairs, axis=1)
        cols = slice(g * gw, (g + 1) * gw)
        xg = x_ref[0, :, cols]
        s_in = st_ref[g]
        y_off = _dot(cg, s_in.astype(BF16)) * ecum_x[:, cols]
        xw = (xg.astype(F32) * w_x[:, cols]).astype(BF16)
        st_ref[g] = s_in * cdec_x[:, cols] + _dot_tn(bg, xw)
        y_ref[0, 0, :, cols] = y_diag + y_off


def ssd_scan(xbc, dt, dtT, a_log, *, ctx_chunks, groups, n_heads, head_dim, n_state):
    bsz, s, _ = xbc.shape
    L = SSD_CHUNK
    nc = s // L
    d_inner = n_heads * head_dim
    gn = groups * n_state
    idx = jnp.arange(L)
    tril = (idx[:, None] >= idx[None, :]).astype(F32)
    tri = jnp.stack([tril, tril.T])
    expand = (jnp.arange(n_heads)[:, None] == (jnp.arange(d_inner)[None, :] // head_dim)).astype(BF16)

    def chunk(d, t):
        bwd = jnp.where(t < ctx_chunks, ctx_chunks - 1 - t, nc - 1 - (t - ctx_chunks))
        return jnp.where(d == 0, t, bwd)

    xb, bb, cbk = 0, d_inner // gn, d_inner // gn + 1
    return pl.pallas_call(
        functools.partial(_ssd_scan_kernel, groups=groups, heads_per_group=n_heads // groups,
                          head_dim=head_dim, n_state=n_state),
        grid=(bsz, 2, nc),
        in_specs=[pl.BlockSpec((1, L, d_inner), lambda b, d, t: (b, chunk(d, t), xb)),
                  pl.BlockSpec((1, L, gn), lambda b, d, t: (b, chunk(d, t), bb)),
                  pl.BlockSpec((1, L, gn), lambda b, d, t: (b, chunk(d, t), cbk)),
                  pl.BlockSpec((1, 1, L, n_heads), lambda b, d, t: (b, d, chunk(d, t), 0)),
                  pl.BlockSpec((1, 1, n_heads, L), lambda b, d, t: (b, d, 0, chunk(d, t))),
                  pl.BlockSpec((1, 1, n_heads), lambda b, d, t: (d, 0, 0)),
                  pl.BlockSpec((1, n_heads, 1), lambda b, d, t: (d, 0, 0)),
                  pl.BlockSpec((1, L, L), lambda b, d, t: (d, 0, 0)),
                  pl.BlockSpec((1, L, L), lambda b, d, t: (1 - d, 0, 0)),
                  pl.BlockSpec((n_heads, d_inner), lambda b, d, t: (0, 0))],
        out_specs=pl.BlockSpec((1, 1, L, d_inner), lambda b, d, t: (d, b, chunk(d, t), 0)),
        out_shape=jax.ShapeDtypeStruct((2, bsz, s, d_inner), F32),
        scratch_shapes=[pltpu.VMEM((groups, n_state, d_inner // groups), F32)],
        compiler_params=_cparams(3),
        name="ssd_scan",
    )(xbc, xbc, xbc, dt, dtT, a_log.reshape(2, 1, n_heads), a_log.reshape(2, n_heads, 1), tri, tri, expand)


def _post_mixer(o, x_ref, g1_ref, sh2_ref, sc2_ref, lng_ref, lnb_ref, rw_ref, rb_ref, xn_ref, tok_ref, lg_ref):
    xn = _layer_norm(DEEPNORM_ALPHA * x_ref[0] + g1_ref[0] * o, lng_ref[...], lnb_ref[...])
    xn_ref[0] = xn
    tok = xn * (1.0 + sc2_ref[0]) + sh2_ref[0]
    tok_ref[...] = tok
    lg_ref[...] = jnp.dot(tok, rw_ref[...], precision=HIGHEST, preferred_element_type=F32) + rb_ref[...]


def _ssd_out_kernel(yf_ref, yb_ref, xa_ref, z_ref, dx_ref, nw_ref, w_ref,
                    x_ref, g1_ref, sh2_ref, sc2_ref, lng_ref, lnb_ref, rw_ref, rb_ref,
                    xn_ref, tok_ref, lg_ref, *, groups):
    z = z_ref[0].astype(F32)
    y = (yf_ref[0, 0] + yb_ref[0, 0] + xa_ref[0].astype(F32) * dx_ref[...]) * (z * _sigmoid(z))
    gw = y.shape[1] // groups
    parts = []
    for g in range(groups):
        yg = y[:, g * gw:(g + 1) * gw]
        parts.append(yg * lax.rsqrt(jnp.mean(yg * yg, axis=-1, keepdims=True) + GATED_NORM_EPS))
    yn = (jnp.concatenate(parts, axis=1) * nw_ref[...]).astype(BF16)
    _post_mixer(_dot(yn, w_ref[...]), x_ref, g1_ref, sh2_ref, sc2_ref, lng_ref, lnb_ref, rw_ref, rb_ref,
                xn_ref, tok_ref, lg_ref)


def _post_specs(d, row, nt):
    ins = [_mk_mod_spec(d, 2, row), _mk_mod_spec(d, 3, row), _mk_mod_spec(d, 4, row),
           pl.BlockSpec((1, d), lambda b, j: (0, 0)), pl.BlockSpec((1, d), lambda b, j: (0, 0)),
           pl.BlockSpec((d, LANES), lambda b, j: (0, 0)), pl.BlockSpec((1, LANES), lambda b, j: (0, 0))]
    outs = [pl.BlockSpec((1, ROW_TILE, d), lambda b, j: (b, j, 0)),
            pl.BlockSpec((ROW_TILE, d), lambda b, j: (b * nt + j, 0)),
            pl.BlockSpec((ROW_TILE, LANES), lambda b, j: (b * nt + j, 0))]
    return ins, outs


def ssd_out(y, xbc, u, xs, mods, d_x, norm_w, w_out, lng, lnb, rw, rb, *, ctx_tiles, ctx_row):
    bsz, s, d = xs.shape
    d_inner = y.shape[3]
    tm = ROW_TILE
    nt = s // tm
    row = lambda b, j: jnp.where(j < ctx_tiles, ctx_row, b)
    post_in, post_out = _post_specs(d, row, nt)
    return pl.pallas_call(
        functools.partial(_ssd_out_kernel, groups=SSD_GROUPS),
        grid=(bsz, nt),
        in_specs=[pl.BlockSpec((1, 1, tm, d_inner), lambda b, j: (0, b, j, 0)),
                  pl.BlockSpec((1, 1, tm, d_inner), lambda b, j: (1, b, j, 0)),
                  pl.BlockSpec((1, tm, d_inner), lambda b, j: (b, j, 0)),
                  pl.BlockSpec((1, tm, d_inner), lambda b, j: (b, j, 0)),
                  pl.BlockSpec((1, d_inner), lambda b, j: (0, 0)),
                  pl.BlockSpec((1, d_inner), lambda b, j: (0, 0)),
                  pl.BlockSpec((d_inner, d), lambda b, j: (0, 0)),
                  pl.BlockSpec((1, tm, d), lambda b, j: (b, j, 0))] + post_in,
        out_specs=post_out,
        out_shape=[jax.ShapeDtypeStruct((bsz, s, d), F32),
                   jax.ShapeDtypeStruct((bsz * s, d), F32),
                   jax.ShapeDtypeStruct((bsz * s, LANES), F32)],
        compiler_params=_cparams(2),
        name="ssd_out",
    )(y, y, xbc, u, d_x, norm_w, w_out, xs, mods, mods, mods, lng, lnb, rw, rb)


def _route_kernel(lg_ref, ls_ref, up_ref, slots_ref, gates_ref, cnt_ref, cnt_s, run_s, offs_s, *, n_exp, tm):
    p, j = pl.program_id(0), pl.program_id(1)
    tr = lg_ref.shape[0]
    lane = lax.broadcasted_iota(I32, (tr, LANES), 1).astype(F32)
    lg = jnp.where(lane < n_exp, lg_ref[...], -jnp.inf)
    onehots, tops = [], []
    for _ in range(TOP_K):
        m = jnp.max(lg, axis=-1, keepdims=True)
        idx = jnp.min(jnp.where(lg == m, lane, float(LANES)), axis=-1, keepdims=True)
        oh = lane == idx
        onehots.append(oh)
        tops.append(m)
        lg = jnp.where(oh, -jnp.inf, lg)
    multi = sum(jnp.where(oh, 1.0, 0.0) for oh in onehots)
    colsum = jnp.sum(multi, axis=0, keepdims=True)

    @pl.when(jnp.logical_and(p == 0, j == 0))
    def _():
        cnt_s[...] = jnp.zeros_like(cnt_s)

    @pl.when(p == 0)
    def _():
        cnt_s[...] += colsum

    @pl.when(p == 1)
    def _():
        @pl.when(j == 0)
        def _():
            cnt = cnt_s[...]
            padded = jnp.ceil(cnt * (1.0 / tm)) * tm
            offs = jnp.dot(jnp.broadcast_to(padded, (8, LANES)), up_ref[...], precision=HIGHEST,
                           preferred_element_type=F32)
            offs_s[...] = offs[0:1]
            run_s[...] = jnp.zeros_like(run_s)
            cnt_ref[...] = jnp.broadcast_to(cnt, (8, LANES))

        before = _dot(ls_ref[...], multi.astype(BF16)) + run_s[...]
        pos = before + offs_s[...]
        slots = jnp.zeros((tr, LANES), F32)
        gates = jnp.zeros((tr, LANES), F32)
        exps = [jnp.exp(t - tops[0]) for t in tops]
        den = sum(exps)
        for k in range(TOP_K):
            sk = jnp.sum(jnp.where(onehots[k], pos, 0.0), axis=-1, keepdims=True)
            slots = jnp.where(lane == k, sk, slots)
            gates = jnp.where(lane == k, exps[k] / den, gates)
        slots_ref[...] = slots.astype(I32)
        gates_ref[...] = gates
        run_s[...] += colsum


def moe_route(logits):
    t = logits.shape[0]
    tr = ROUTE_TILE
    nt = t // tr
    idx = jnp.arange(tr)
    strict_lower = (idx[:, None] > idx[None, :]).astype(BF16)
    ie = jnp.arange(LANES)
    strict_upper = (ie[:, None] < ie[None, :]).astype(F32)
    return pl.pallas_call(
        functools.partial(_route_kernel, n_exp=N_EXPERTS, tm=MOE_TILE),
        grid=(2, nt),
        in_specs=[pl.BlockSpec((tr, LANES), lambda p, j: (j, 0)),
                  pl.BlockSpec((tr, tr), lambda p, j: (0, 0)),
                  pl.BlockSpec((LANES, LANES), lambda p, j: (0, 0))],
        out_specs=[pl.BlockSpec((tr, LANES), lambda p, j: (j * p, 0)),
                   pl.BlockSpec((tr, LANES), lambda p, j: (j * p, 0)),
                   pl.BlockSpec((8, LANES), lambda p, j: (0, 0))],
        out_shape=[jax.ShapeDtypeStruct((t, LANES), I32),
                   jax.ShapeDtypeStruct((t, LANES), F32),
                   jax.ShapeDtypeStruct((8, LANES), F32)],
        scratch_shapes=[pltpu.VMEM((1, LANES), F32)] * 3,
        compiler_params=_cparams(2),
        name="moe_route",
    )(logits, strict_lower, strict_upper)


def _row_copy(src_ref, src_row, dst_ref, dst_row, sem):
    return pltpu.make_async_copy(src_ref.at[pl.ds(src_row, 1)], dst_ref.at[pl.ds(dst_row, 1)], sem)


def _dispatch_kernel(pad_start_ref, pad_len_ref, slots_ref, tok_ref, xs_ref, zrow, sem, *, n_exp):
    i = pl.program_id(0)
    td = tok_ref.shape[0]

    @pl.when(i == 0)
    def _():
        zrow[...] = jnp.zeros_like(zrow)
        for start in (True, False):
            def per_expert(e, c, start=start):
                def per_row(r, c2):
                    cp = _row_copy(zrow, 0, xs_ref, pad_start_ref[e] + r, sem)
                    cp.start() if start else cp.wait()
                    return c2
                return lax.fori_loop(0, pad_len_ref[e], per_row, c)
            lax.fori_loop(0, n_exp, per_expert, 0)

    for start in (True, False):
        def per_token(t, c, start=start):
            for k in range(TOP_K):
                cp = _row_copy(tok_ref, t, xs_ref, slots_ref[TOP_K * t + k], sem)
                cp.start() if start else cp.wait()
            return c
        lax.fori_loop(0, td, per_token, 0)


def moe_dispatch(tok, slots_flat, pad_start, pad_len, n_slots):
    t, d = tok.shape
    td = ROW_TILE
    return pl.pallas_call(
        functools.partial(_dispatch_kernel, n_exp=N_EXPERTS),
        grid_spec=pltpu.PrefetchScalarGridSpec(
            num_scalar_prefetch=2,
            grid=(t // td,),
            in_specs=[pl.BlockSpec((td * TOP_K,), lambda i, *_: (i,), memory_space=pltpu.SMEM),
                      pl.BlockSpec((td, d), lambda i, *_: (i, 0))],
            out_specs=pl.BlockSpec(memory_space=pl.ANY),
            scratch_shapes=[pltpu.VMEM((8, d), F32), pltpu.SemaphoreType.DMA(())]),
        out_shape=jax.ShapeDtypeStruct((n_slots, d), F32),
        compiler_params=_cparams(1),
        name="moe_dispatch",
    )(pad_start, pad_len, slots_flat, tok)


def _experts_kernel(be_ref, nu_ref, x_ref, w1_ref, b1_ref, w2_ref, b2_ref, o_ref, w1_s, w2_s):
    b = pl.program_id(0)
    prev = be_ref[jnp.maximum(b - 1, 0)]

    @pl.when(b < nu_ref[0])
    def _():
        @pl.when(jnp.logical_or(b == 0, be_ref[b] != prev))
        def _():
            w1_s[...] = w1_ref[0].astype(BF16)
            w2_s[...] = w2_ref[0].astype(BF16)

        u = _dot(x_ref[...].astype(BF16), w1_s[...]) + b1_ref[0]
        f = u.shape[1] // 2
        glu = jnp.minimum(u[:, :f], SWIGLU_LIMIT)
        lin = jnp.clip(u[:, f:], -SWIGLU_LIMIT, SWIGLU_LIMIT)
        act = (glu * _sigmoid(SWIGLU_ALPHA * glu) * (lin + 1.0)).astype(BF16)
        o_ref[...] = _dot(act, w2_s[...]) + b2_ref[0]


def moe_experts(xs, block_e, n_used, w_in, b_in, w_out, b_out):
    n_slots, d = xs.shape
    n_exp, _, f2 = w_in.shape
    tm = MOE_TILE
    blk = lambda b, be, nu: (jnp.minimum(b, nu[0] - 1), 0)
    exp = lambda b, be, nu: (be[b], 0, 0)
    return pl.pallas_call(
        _experts_kernel,
        grid_spec=pltpu.PrefetchScalarGridSpec(
            num_scalar_prefetch=2,
            grid=(n_slots // tm,),
            in_specs=[pl.BlockSpec((tm, d), blk),
                      pl.BlockSpec((1, d, f2), exp),
                      pl.BlockSpec((1, 1, f2), exp),
                      pl.BlockSpec((1, f2 // 2, d), exp),
                      pl.BlockSpec((1, 1, d), exp)],
            out_specs=pl.BlockSpec((tm, d), blk),
            scratch_shapes=[pltpu.VMEM((d, f2), BF16), pltpu.VMEM((f2 // 2, d), BF16)]),
        out_shape=jax.ShapeDtypeStruct((n_slots, d), F32),
        compiler_params=_cparams(1, VMEM_LIMIT_MOE),
        name="moe_experts",
    )(block_e, n_used, xs, w_in, b_in.reshape(n_exp, 1, f2), w_out, b_out.reshape(n_exp, 1, d))


def _combine_kernel(s_cur_ref, s_nxt_ref, gates_ref, x_ref, g2_ref, lng_ref, lnb_ref, ys_ref, o_ref,
                    buf, sem, *, n_tiles):
    i = pl.program_id(0)
    tc = x_ref.shape[1]

    def gather(slot_ref, s, start):
        def per_token(t, c):
            for k in range(TOP_K):
                row = slot_ref[TOP_K * t + k] if start else 0
                cp = pltpu.make_async_copy(ys_ref.at[pl.ds(row, 1)], buf.at[s, k, pl.ds(t, 1)], sem.at[s])
                cp.start() if start else cp.wait()
            return c
        lax.fori_loop(0, tc, per_token, 0)

    @pl.when(i == 0)
    def _():
        gather(s_cur_ref, 0, True)

    @pl.when(i + 1 < n_tiles)
    def _():
        gather(s_nxt_ref, (i + 1) % 2, True)

    s = i % 2
    gather(s_cur_ref, s, False)
    g = gates_ref[...]
    f = g[:, 0:1] * buf[s, 0]
    for k in range(1, TOP_K):
        f = f + g[:, k:k + 1] * buf[s, k]
    o_ref[0] = _layer_norm(DEEPNORM_ALPHA * x_ref[0] + g2_ref[0] * f, lng_ref[...], lnb_ref[...])


def moe_combine(ys, slots_flat, gates, xn, mods, lng, lnb, *, tiles_per_b, tile0, row_fn):
    bsz, _, d = xn.shape
    tc = ROW_TILE
    n_tiles = bsz * tiles_per_b
    where = lambda i: (i // tiles_per_b, i % tiles_per_b + tile0, 0)
    return pl.pallas_call(
        functools.partial(_combine_kernel, n_tiles=n_tiles),
        grid=(n_tiles,),
        in_specs=[pl.BlockSpec((tc * TOP_K,), lambda i: (i,), memory_space=pltpu.SMEM),
                  pl.BlockSpec((tc * TOP_K,), lambda i: (jnp.minimum(i + 1, n_tiles - 1),), memory_space=pltpu.SMEM),
                  pl.BlockSpec((tc, LANES), lambda i: (i, 0)),
                  pl.BlockSpec((1, tc, d), where),
                  pl.BlockSpec((1, 1, d), lambda i: (row_fn(i // tiles_per_b, i % tiles_per_b + tile0) * 6 + 5, 0, 0)),
                  pl.BlockSpec((1, d), lambda i: (0, 0)),
                  pl.BlockSpec((1, d), lambda i: (0, 0)),
                  pl.BlockSpec(memory_space=pl.ANY)],
        out_specs=pl.BlockSpec((1, tc, d), lambda i: (i // tiles_per_b, i % tiles_per_b, 0)),
        out_shape=jax.ShapeDtypeStruct((bsz, tiles_per_b * tc, d), F32),
        scratch_shapes=[pltpu.VMEM((2, TOP_K, tc, d), F32), pltpu.SemaphoreType.DMA((2,))],
        compiler_params=_cparams(1),
        name="moe_combine",
    )(slots_flat, slots_flat, gates, xn, mods, lng, lnb, ys)


def moe_ffn(tok, logits, w_in, b_in, w_out, b_out):
    t = tok.shape[0]
    tm = MOE_TILE
    n_blocks = (t * TOP_K + N_EXPERTS * (tm - 1) + tm - 1) // tm
    slots, gates, cnt = moe_route(logits)
    counts = cnt[0, :N_EXPERTS].astype(I32)
    padded = (counts + tm - 1) // tm * tm
    padded_end = jnp.cumsum(padded)
    pad_start = padded_end - padded + counts
    n_used = (padded_end[-1:] // tm).astype(I32)
    block_e = jnp.minimum(jnp.searchsorted(padded_end, jnp.arange(n_blocks) * tm, side='right'),
                          N_EXPERTS - 1).astype(I32)
    slots_flat = slots[:, :TOP_K].reshape(-1)
    xs = moe_dispatch(tok, slots_flat, pad_start.astype(I32), (padded - counts).astype(I32), n_blocks * tm)
    ys = moe_experts(xs, block_e, n_used, w_in, b_in, w_out, b_out)
    return ys, slots_flat, gates


def _mla_proj_kernel(*refs, with_q):
    if with_q:
        (x_ref, sc_ref, sh_ref, wa_ref, qn_ref, kvn_ref, wq_ref, wkv_ref, cos_ref, sin_ref,
         qn_out, qr_out, kv_out, kr_out) = refs
    else:
        x_ref, sc_ref, sh_ref, wa_ref, kvn_ref, wkv_ref, kv_out, kr_out = refs
    h = (x_ref[0] * (1.0 + sc_ref[0]) + sh_ref[0]).astype(BF16)
    a = _dot(h, wa_ref[...])
    q0, kv0 = 0, MLA_Q_LORA
    kr0 = MLA_Q_LORA + MLA_KV_LORA
    kp0 = kr0 + LANES
    kvn = _rms_norm(a[:, kv0:kv0 + MLA_KV_LORA], kvn_ref[...]).astype(BF16)
    kv_out[0] = _dot(kvn, wkv_ref[...]).astype(BF16)
    k_rope = a[:, kr0:kr0 + MLA_ROPE]
    if not with_q:
        kr_out[0] = k_rope.astype(BF16)
        return
    cos, sin = cos_ref[...], sin_ref[...]
    kr_out[0] = (k_rope * cos + a[:, kp0:kp0 + MLA_ROPE] * sin).astype(BF16)
    qn = _rms_norm(a[:, q0:q0 + MLA_Q_LORA], qn_ref[...]).astype(BF16)
    q = _dot(qn, wq_ref[...])
    n_nope = MLA_HEADS * MLA_NOPE
    qn_out[0] = (q[:, 0:n_nope] * MLA_SCALE).astype(BF16)
    for hd in range(MLA_HEADS):
        r0 = n_nope + hd * LANES
        p0 = n_nope + (MLA_HEADS + hd) * LANES
        qr = q[:, r0:r0 + MLA_ROPE] * cos + q[:, p0:p0 + MLA_ROPE] * sin
        qr_out[0, hd] = (qr * MLA_SCALE).astype(BF16)


def mla_proj(xs, mods, wa, q_norm, kv_norm, wq, wkv, cos, sin, *, tile0, n_rows, row_fn, with_q):
    bsz, _, d = xs.shape
    tm = ROW_TILE
    nt = n_rows // tm
    full = lambda shape: pl.BlockSpec(shape, lambda b, j: (0,) * len(shape))
    row = lambda b, j: row_fn(b, j + tile0)
    in_specs = [pl.BlockSpec((1, tm, d), lambda b, j: (b, j + tile0, 0)),
                _mk_mod_spec(d, 1, row), _mk_mod_spec(d, 0, row), full(wa.shape)]
    args = [xs, mods, mods, wa]
    kv_w = wkv.shape[1]
    kv_specs = [pl.BlockSpec((1, tm, kv_w), lambda b, j: (b, j, 0)),
                pl.BlockSpec((1, tm, MLA_ROPE), lambda b, j: (b, j, 0))]
    kv_shapes = [jax.ShapeDtypeStruct((bsz, n_rows, kv_w), BF16),
                 jax.ShapeDtypeStruct((bsz, n_rows, MLA_ROPE), BF16)]
    if with_q:
        in_specs += [full(q_norm.shape), full(kv_norm.shape), full(wq.shape), full(wkv.shape),
                     pl.BlockSpec((tm, MLA_ROPE), lambda b, j: (j, 0)),
                     pl.BlockSpec((tm, MLA_ROPE), lambda b, j: (j, 0))]
        args += [q_norm, kv_norm, wq, wkv, cos, sin]
        n_nope = MLA_HEADS * MLA_NOPE
        out_specs = [pl.BlockSpec((1, tm, n_nope), lambda b, j: (b, j, 0)),
                     pl.BlockSpec((1, MLA_HEADS, tm, MLA_ROPE), lambda b, j: (b, 0, j, 0))] + kv_specs
        out_shape = [jax.ShapeDtypeStruct((bsz, n_rows, n_nope), BF16),
                     jax.ShapeDtypeStruct((bsz, MLA_HEADS, n_rows, MLA_ROPE), BF16)] + kv_shapes
    else:
        in_specs += [full(kv_norm.shape), full(wkv.shape)]
        args += [kv_norm, wkv]
        out_specs, out_shape = kv_specs, kv_shapes
    return pl.pallas_call(
        functools.partial(_mla_proj_kernel, with_q=with_q),
        grid=(bsz, nt),
        in_specs=in_specs, out_specs=out_specs, out_shape=out_shape,
        compiler_params=_cparams(2),
        name="mla_proj_q" if with_q else "mla_proj_ctx",
    )(*args)


def _attn_kernel(qn_ref, qr_ref, knc_ref, knl_ref, vc_ref, vl_ref, krc_ref, krl_ref, o_ref, kcat, qcat,
                 *, n_ctx, kc):
    j = pl.program_id(2)
    n_lat = knl_ref.shape[1]
    rope_end = MLA_NOPE + MLA_ROPE

    @pl.when(j == 0)
    def _():
        kcat[0:n_ctx, 0:MLA_NOPE] = knc_ref[0]
        kcat[n_ctx:n_ctx + n_lat, 0:MLA_NOPE] = knl_ref[0]
        kcat[0:n_ctx, MLA_NOPE:rope_end] = krc_ref[0]
        kcat[n_ctx:n_ctx + n_lat, MLA_NOPE:rope_end] = krl_ref[0]
        kcat[:, rope_end:] = jnp.zeros((kcat.shape[0], kcat.shape[1] - rope_end), BF16)

    tq = qn_ref.shape[1]
    qcat[:, 0:MLA_NOPE] = qn_ref[0]
    qcat[:, MLA_NOPE:rope_end] = qr_ref[0, 0]
    qcat[:, rope_end:] = jnp.zeros((tq, qcat.shape[1] - rope_end), BF16)
    q = qcat[...]
    m = jnp.full((tq, 1), -jnp.inf, F32)
    l = jnp.zeros((tq, 1), F32)
    acc = jnp.zeros((tq, MLA_V), F32)
    chunks = [(0, n_ctx, vc_ref, 0)] + [(n_ctx + c0, kc, vl_ref, c0) for c0 in range(0, n_lat, kc)]
    for k0, kn, v_ref, v0 in chunks:
        s = _dot_nt(q, kcat[k0:k0 + kn])
        m_new = jnp.maximum(m, jnp.max(s, axis=-1, keepdims=True))
        alpha = jnp.exp(m - m_new)
        p = jnp.exp(s - m_new)
        l = alpha * l + jnp.sum(p, axis=-1, keepdims=True)
        acc = alpha * acc + _dot(p.astype(BF16), v_ref[0, v0:v0 + kn])
        m = m_new
    o_ref[0] = (acc / l).astype(BF16)


def mla_attention(qn, qr, kv_c, kv_l, kr_c, kr_l):
    bsz, n_lat, _ = qn.shape
    n_ctx = kv_c.shape[1]
    tq = ATTN_Q_TILE
    kc = min(ATTN_K_CHUNK, n_lat)
    cat_w = 2 * LANES
    return pl.pallas_call(
        functools.partial(_attn_kernel, n_ctx=n_ctx, kc=kc),
        grid=(bsz, MLA_HEADS, n_lat // tq),
        in_specs=[pl.BlockSpec((1, tq, MLA_NOPE), lambda b, h, j: (b, j, h)),
                  pl.BlockSpec((1, 1, tq, MLA_ROPE), lambda b, h, j: (b, h, j, 0)),
                  pl.BlockSpec((1, n_ctx, MLA_NOPE), lambda b, h, j: (b, 0, 2 * h)),
                  pl.BlockSpec((1, n_lat, MLA_NOPE), lambda b, h, j: (b, 0, 2 * h)),
                  pl.BlockSpec((1, n_ctx, MLA_V), lambda b, h, j: (b, 0, 2 * h + 1)),
                  pl.BlockSpec((1, n_lat, MLA_V), lambda b, h, j: (b, 0, 2 * h + 1)),
                  pl.BlockSpec((1, n_ctx, MLA_ROPE), lambda b, h, j: (b, 0, 0)),
                  pl.BlockSpec((1, n_lat, MLA_ROPE), lambda b, h, j: (b, 0, 0))],
        out_specs=pl.BlockSpec((1, tq, MLA_V), lambda b, h, j: (b, j, h)),
        out_shape=jax.ShapeDtypeStruct((bsz, n_lat, MLA_HEADS * MLA_V), BF16),
        scratch_shapes=[pltpu.VMEM((n_ctx + n_lat, cat_w), BF16), pltpu.VMEM((tq, cat_w), BF16)],
        compiler_params=_cparams(3),
        name="mla_attention",
    )(qn, qr, kv_c, kv_l, kv_c, kv_l, kr_c, kr_l)


def _mla_out_kernel(o_ref, w_ref, x_ref, g1_ref, sh2_ref, sc2_ref, lng_ref, lnb_ref, rw_ref, rb_ref,
                    xn_ref, tok_ref, lg_ref):
    _post_mixer(_dot(o_ref[0], w_ref[...]), x_ref, g1_ref, sh2_ref, sc2_ref, lng_ref, lnb_ref, rw_ref, rb_ref,
                xn_ref, tok_ref, lg_ref)


def mla_out(o, w_o, xs, mods, lng, lnb, rw, rb, *, tile0, row_fn):
    bsz, n_rows, dv = o.shape
    d = xs.shape[2]
    tm = ROW_TILE
    nt = n_rows // tm
    row = lambda b, j: row_fn(b, j + tile0)
    post_in, post_out = _post_specs(d, row, nt)
    return pl.pallas_call(
        _mla_out_kernel,
        grid=(bsz, nt),
        in_specs=[pl.BlockSpec((1, tm, dv), lambda b, j: (b, j, 0)),
                  pl.BlockSpec((dv, d), lambda b, j: (0, 0)),
                  pl.BlockSpec((1, tm, d), lambda b, j: (b, j + tile0, 0))] + post_in,
        out_specs=post_out,
        out_shape=[jax.ShapeDtypeStruct((bsz, n_rows, d), F32),
                   jax.ShapeDtypeStruct((bsz * n_rows, d), F32),
                   jax.ShapeDtypeStruct((bsz * n_rows, LANES), F32)],
        compiler_params=_cparams(2),
        name="mla_out",
    )(o, w_o, xs, mods, mods, mods, lng, lnb, rw, rb)


def _rope_tables(seq_len):
    t = jnp.arange(seq_len)
    row = (t // GRID_W).astype(F32)
    col = (t % GRID_W).astype(F32)
    inv = ROPE_BASE ** (-jnp.arange(ROPE_FREQS, dtype=F32) / ROPE_FREQS)
    ang = jnp.stack([row[:, None] * inv, col[:, None] * inv], axis=1)
    cos = jnp.broadcast_to(jnp.cos(ang)[:, :, None, :], (seq_len, 2, 2, ROPE_FREQS)).reshape(seq_len, MLA_ROPE)
    sign = jnp.array([-1.0, 1.0], F32)[None, None, :, None]
    sin = (jnp.sin(ang)[:, :, None, :] * sign).reshape(seq_len, MLA_ROPE)
    return cos, sin


def _rope_partner_perm():
    j = jnp.arange(MLA_ROPE)
    half = (j // ROPE_FREQS) % 2
    return jnp.where(half == 0, j + ROPE_FREQS, j - ROPE_FREQS)


def _pad_cols(w, n):
    return jnp.pad(w, ((0, 0), (0, n - w.shape[1])))


def _router_pad(rw, rb):
    return _pad_cols(rw, LANES), _pad_cols(rb[None, :], LANES)


def kernel(x, c, ctx, c_ctx, mod_w, mod_b, ln1_g, ln1_b, ln2_g, ln2_b, ssd_w_in, ssd_conv_w, ssd_conv_b, ssd_dt_bias, ssd_a_log, ssd_d, ssd_norm_w, ssd_w_out, mla_w_a, mla_q_norm, mla_kv_norm, mla_w_qb, mla_w_kvb, mla_w_o, router_w, router_b, moe_w_in, moe_b_in, moe_w_out, moe_b_out):
    bsz, n_lat, d = x.shape
    n_ctx = ctx.shape[1]
    assert mod_w.shape[0] == DEPTH == 2 and bsz + 1 <= MOD_ROWS
    assert n_ctx % ROW_TILE == 0 and n_lat % ROUTE_TILE == 0 and (bsz * n_ctx) % ROUTE_TILE == 0
    ctx_tiles = n_ctx // ROW_TILE
    lat_tiles = n_lat // ROW_TILE
    ctx_row = bsz
    row_fn = lambda b, j: jnp.where(j < ctx_tiles, ctx_row, b)

    cond = jnp.zeros((MOD_ROWS, d), F32).at[:bsz].set(c).at[ctx_row].set(c_ctx)
    mods = adaln_mods(cond, mod_w, mod_b).reshape(DEPTH, MOD_ROWS * 6, 1, d)
    xs = jnp.concatenate([ctx, x], axis=1)

    n_heads = ssd_a_log.shape[2]
    d_inner = n_heads * SSD_HEAD_DIM
    conv_dim = ssd_conv_w.shape[2]
    w_in = ssd_w_in[0]
    w_zx = w_in[:, :d_inner + conv_dim].astype(BF16)
    w_dt = w_in[:, d_inner + conv_dim:]
    u, dt, dtT = ssd_in_proj(xs, mods[0], w_zx, _pad_cols(w_dt, LANES).astype(BF16), w_dt.T.astype(BF16),
                             _pad_cols(ssd_dt_bias[0].reshape(1, -1), LANES), ssd_dt_bias[0].reshape(-1, 1),
                             ctx_tiles=ctx_tiles, ctx_row=ctx_row, n_heads=n_heads)
    xbc = ssd_conv(u, jnp.pad(ssd_conv_w[0], ((0, 8 - SSD_CONV), (0, 0))), ssd_conv_b[0][None, :],
                   col0=d_inner, ctx_tiles=ctx_tiles)
    y = ssd_scan(xbc, dt, dtT, ssd_a_log[0], ctx_chunks=n_ctx // SSD_CHUNK, groups=SSD_GROUPS,
                 n_heads=n_heads, head_dim=SSD_HEAD_DIM, n_state=SSD_STATE)
    rw0, rb0 = _router_pad(router_w[0], router_b[0])
    xn, tok, logits = ssd_out(y, xbc, u, xs, mods[0], jnp.repeat(ssd_d[0], SSD_HEAD_DIM)[None, :],
                              ssd_norm_w[0][None, :], ssd_w_out[0].astype(BF16),
                              ln1_g[0][None, :], ln1_b[0][None, :], rw0, rb0, ctx_tiles=ctx_tiles, ctx_row=ctx_row)
    ys, slots, gates = moe_ffn(tok, logits, moe_w_in[0], moe_b_in[0], moe_w_out[0], moe_b_out[0])
    xs = moe_combine(ys, slots, gates, xn, mods[0], ln2_g[0][None, :], ln2_b[0][None, :],
                     tiles_per_b=ctx_tiles + lat_tiles, tile0=0, row_fn=row_fn)

    perm = _rope_partner_perm()
    wa = mla_w_a[0]
    kr0 = MLA_Q_LORA + MLA_KV_LORA
    w_kr = wa[:, kr0:kr0 + MLA_ROPE]
    wa_ext = jnp.concatenate([wa[:, :kr0], _pad_cols(w_kr, LANES), _pad_cols(w_kr[:, perm], LANES)], axis=1).astype(BF16)
    wq = mla_w_qb[0].reshape(MLA_Q_LORA, MLA_HEADS, MLA_NOPE + MLA_ROPE)
    wq_rope = wq[:, :, MLA_NOPE:]
    pad_heads = lambda w: jnp.pad(w, ((0, 0), (0, 0), (0, LANES - MLA_ROPE))).reshape(MLA_Q_LORA, MLA_HEADS * LANES)
    wq_ext = jnp.concatenate([wq[:, :, :MLA_NOPE].reshape(MLA_Q_LORA, MLA_HEADS * MLA_NOPE),
                              pad_heads(wq_rope), pad_heads(wq_rope[:, :, perm])], axis=1).astype(BF16)
    wkv = mla_w_kvb[0].astype(BF16)
    cos, sin = _rope_tables(n_lat)
    q_norm, kv_norm = mla_q_norm[0][None, :], mla_kv_norm[0][None, :]
    qn, qr, kv_l, kr_l = mla_proj(xs, mods[1], wa_ext, q_norm, kv_norm, wq_ext, wkv, cos, sin,
                                  tile0=ctx_tiles, n_rows=n_lat, row_fn=row_fn, with_q=True)
    kv_c, kr_c = mla_proj(xs, mods[1], wa_ext, q_norm, kv_norm, wq_ext, wkv, cos, sin,
                          tile0=0, n_rows=n_ctx, row_fn=row_fn, with_q=False)
    o = mla_attention(qn, qr, kv_c, kv_l, kr_c, kr_l)
    rw1, rb1 = _router_pad(router_w[1], router_b[1])
    xn, tok, logits = mla_out(o, mla_w_o[0].astype(BF16), xs, mods[1], ln1_g[1][None, :], ln1_b[1][None, :],
                              rw1, rb1, tile0=ctx_tiles, row_fn=row_fn)
    ys, slots, gates = moe_ffn(tok, logits, moe_w_in[1], moe_b_in[1], moe_w_out[1], moe_b_out[1])
    lat_row = lambda b, j: b
    return moe_combine(ys, slots, gates, xn, mods[1], ln2_g[1][None, :], ln2_b[1][None, :],
                       tiles_per_b=lat_tiles, tile0=0, row_fn=lat_row)
```

```python
import functools
import math

import jax
import jax.numpy as jnp
from jax import lax
from jax.experimental import pallas as pl
from jax.experimental.pallas import tpu as pltpu

F32, BF16, I32 = jnp.float32, jnp.bfloat16, jnp.int32
HIGHEST = lax.Precision.HIGHEST

DEPTH = 2
GRID_W = 64
DEEPNORM_ALPHA = (2 * DEPTH) ** 0.25
LN_EPS = 1e-5
RMS_EPS = 1e-6
GATED_NORM_EPS = 1e-5
SSD_HEAD_DIM = 64
SSD_GROUPS = 8
SSD_STATE = 128
SSD_CONV = 5
SSD_CHUNK = 128
MLA_HEADS = 8
MLA_NOPE = 128
MLA_ROPE = 64
MLA_V = 128
MLA_Q_LORA = 384
MLA_KV_LORA = 256
MLA_SCALE = (MLA_NOPE + MLA_ROPE) ** -0.5
ROPE_FREQS = MLA_ROPE // 4
ROPE_BASE = 10000.0
N_EXPERTS = 32
TOP_K = 4
SWIGLU_LIMIT = 7.0
SWIGLU_ALPHA = 1.702

LANES = 128
ROW_TILE = 256
MOE_TILE = 256
ROUTE_TILE = 512
ATTN_Q_TILE = 512
ATTN_K_CHUNK = 512
HALO = 16
MOD_ROWS = 8
VMEM_LIMIT = 48 * 1024 * 1024
VMEM_LIMIT_MOE = 56 * 1024 * 1024


def _cparams(n_axes, vmem=VMEM_LIMIT):
    return pltpu.CompilerParams(dimension_semantics=("arbitrary",) * n_axes, vmem_limit_bytes=vmem)


def _sigmoid(v):
    return 1.0 / (1.0 + jnp.exp(-v))


def _softplus(v):
    return jnp.maximum(v, 0.0) + jnp.log1p(jnp.exp(-jnp.abs(v)))


def _layer_norm(v, g, b):
    mu = jnp.mean(v, axis=-1, keepdims=True)
    d = v - mu
    var = jnp.mean(d * d, axis=-1, keepdims=True)
    return d * lax.rsqrt(var + LN_EPS) * g + b


def _rms_norm(v, g):
    return v * lax.rsqrt(jnp.mean(v * v, axis=-1, keepdims=True) + RMS_EPS) * g


def _dot(a, b):
    return jnp.dot(a, b, preferred_element_type=F32)


def _dot_nt(a, b):
    return lax.dot_general(a, b, (((1,), (1,)), ((), ())), preferred_element_type=F32)


def _dot_tn(a, b):
    return lax.dot_general(a, b, (((0,), (0,)), ((), ())), preferred_element_type=F32)


def _split_bf16(v):
    hi = v.astype(BF16)
    lo = (v - hi.astype(F32)).astype(BF16)
    return hi, lo


def _mod_kernel(c_ref, w_ref, b_ref, o_ref):
    c = c_ref[...]
    s = (c * _sigmoid(c)).astype(BF16)
    o_ref[0] = _dot(s, w_ref[0].astype(BF16)) + b_ref[0]


def adaln_mods(cond, mod_w, mod_b):
    depth, d, n = mod_w.shape
    tn = n // 4
    return pl.pallas_call(
        _mod_kernel,
        grid=(depth, n // tn),
        in_specs=[pl.BlockSpec((MOD_ROWS, d), lambda l, j: (0, 0)),
                  pl.BlockSpec((1, d, tn), lambda l, j: (l, 0, j)),
                  pl.BlockSpec((1, 1, tn), lambda l, j: (l, 0, j))],
        out_specs=pl.BlockSpec((1, MOD_ROWS, tn), lambda l, j: (l, 0, j)),
        out_shape=jax.ShapeDtypeStruct((depth, MOD_ROWS, n), F32),
        compiler_params=_cparams(2),
        name="adaln_mods",
    )(cond, mod_w, mod_b.reshape(depth, 1, n))


def _mk_mod_spec(d, which, row_fn):
    return pl.BlockSpec((1, 1, d), lambda *idx: (row_fn(*idx) * 6 + which, 0, 0))


def _ssd_in_kernel(x_ref, sc_ref, sh_ref, w_ref, wdt_ref, wdtT_ref, bias_ref, biasT_ref,
                   u_ref, dt_ref, dtT_ref, *, n_heads):
    h = (x_ref[0] * (1.0 + sc_ref[0]) + sh_ref[0]).astype(BF16)
    n_out = u_ref.shape[2]
    step = 512
    for n in range(n_out // step):
        u_ref[0, :, n * step:(n + 1) * step] = _dot(h, w_ref[:, n * step:(n + 1) * step]).astype(BF16)
    dt = _softplus(_dot(h, wdt_ref[...]) + bias_ref[...])
    dt_ref[0, 0] = dt[:, 0:n_heads]
    dt_ref[0, 1] = dt[:, n_heads:2 * n_heads]
    dtT = _softplus(_dot_nt(wdtT_ref[...], h) + biasT_ref[...])
    dtT_ref[0, 0] = dtT[0:n_heads]
    dtT_ref[0, 1] = dtT[n_heads:2 * n_heads]


def ssd_in_proj(xs, mods, w_zx, w_dt, w_dtT, dt_bias, dt_biasT, *, ctx_tiles, ctx_row, n_heads):
    bsz, s, d = xs.shape
    n_out = w_zx.shape[1]
    tm = ROW_TILE
    row = lambda b, j: jnp.where(j < ctx_tiles, ctx_row, b)
    return pl.pallas_call(
        functools.partial(_ssd_in_kernel, n_heads=n_heads),
        grid=(bsz, s // tm),
        in_specs=[pl.BlockSpec((1, tm, d), lambda b, j: (b, j, 0)),
                  _mk_mod_spec(d, 1, row), _mk_mod_spec(d, 0, row),
                  pl.BlockSpec((d, n_out), lambda b, j: (0, 0)),
                  pl.BlockSpec((d, LANES), lambda b, j: (0, 0)),
                  pl.BlockSpec((2 * n_heads, d), lambda b, j: (0, 0)),
                  pl.BlockSpec((1, LANES), lambda b, j: (0, 0)),
                  pl.BlockSpec((2 * n_heads, 1), lambda b, j: (0, 0))],
        out_specs=[pl.BlockSpec((1, tm, n_out), lambda b, j: (b, j, 0)),
                   pl.BlockSpec((1, 2, tm, n_heads), lambda b, j: (b, 0, j, 0)),
                   pl.BlockSpec((1, 2, n_heads, tm), lambda b, j: (b, 0, 0, j))],
        out_shape=[jax.ShapeDtypeStruct((bsz, s, n_out), BF16),
                   jax.ShapeDtypeStruct((bsz, 2, s, n_heads), F32),
                   jax.ShapeDtypeStruct((bsz, 2, n_heads, s), F32)],
        compiler_params=_cparams(2),
        name="ssd_in_proj",
    )(xs, mods, mods, w_zx, w_dt, w_dtT, dt_bias, dt_biasT)


def _ssd_conv_kernel(prev_ref, cur_ref, next_ref, w_ref, b_ref, o_ref, scr, *, tc, ctx_tiles, n_tiles):
    j = pl.program_id(1)
    has_prev = jnp.logical_and(j != 0, j != ctx_tiles)
    has_next = jnp.logical_and(j != ctx_tiles - 1, j != n_tiles - 1)
    scr[0:HALO] = jnp.where(has_prev, prev_ref[0].astype(F32), 0.0)
    scr[HALO:HALO + tc] = cur_ref[0].astype(F32)
    scr[HALO + tc:2 * HALO + tc] = jnp.where(has_next, next_ref[0].astype(F32), 0.0)
    pad = SSD_CONV // 2
    acc = b_ref[...] + w_ref[0:1] * scr[HALO - pad:HALO - pad + tc]
    for k in range(1, SSD_CONV):
        acc = acc + w_ref[k:k + 1] * scr[HALO - pad + k:HALO - pad + k + tc]
    o_ref[0] = (acc * _sigmoid(acc)).astype(BF16)


def ssd_conv(u, conv_w, conv_b, *, col0, ctx_tiles):
    bsz, s, _ = u.shape
    c = conv_w.shape[1]
    tc, wc = ROW_TILE, 2048
    n_tiles = s // tc
    cb0 = col0 // wc
    hb = tc // HALO
    return pl.pallas_call(
        functools.partial(_ssd_conv_kernel, tc=tc, ctx_tiles=ctx_tiles, n_tiles=n_tiles),
        grid=(bsz, n_tiles, c // wc),
        in_specs=[pl.BlockSpec((1, HALO, wc), lambda b, j, k: (b, jnp.maximum(j * hb - 1, 0), cb0 + k)),
                  pl.BlockSpec((1, tc, wc), lambda b, j, k: (b, j, cb0 + k)),
                  pl.BlockSpec((1, HALO, wc), lambda b, j, k: (b, jnp.minimum((j + 1) * hb, s // HALO - 1), cb0 + k)),
                  pl.BlockSpec((8, wc), lambda b, j, k: (0, k)),
                  pl.BlockSpec((1, wc), lambda b, j, k: (0, k))],
        out_specs=pl.BlockSpec((1, tc, wc), lambda b, j, k: (b, j, k)),
        out_shape=jax.ShapeDtypeStruct((bsz, s, c), BF16),
        scratch_shapes=[pltpu.VMEM((tc + 2 * HALO, wc), F32)],
        compiler_params=_cparams(3),
        name="ssd_conv",
    )(u, u, u, conv_w, conv_b)


def _ssd_scan_kernel(x_ref, b_ref, c_ref, dt_ref, dtT_ref, alog_ref, alogT_ref, tri_ref, triT_ref, e_ref,
                     y_ref, st_ref, *, groups, heads_per_group, head_dim, n_state):
    step = pl.program_id(2)

    @pl.when(step == 0)
    def _():
        st_ref[...] = jnp.zeros_like(st_ref)

    L = x_ref.shape[1]
    n_heads = groups * heads_per_group
    gw = heads_per_group * head_dim
    dt = dt_ref[0, 0]
    dtT = dtT_ref[0, 0]
    la = dt * (-jnp.exp(alog_ref[0]))
    laT = dtT * (-jnp.exp(alogT_ref[0]))
    tri = tri_ref[0]
    cum = jnp.dot(tri, la, precision=HIGHEST, preferred_element_type=F32)
    cumT = jnp.dot(laT, triT_ref[0], precision=HIGHEST, preferred_element_type=F32)
    tot = jnp.sum(la, axis=0, keepdims=True)
    ecum = jnp.exp(cum)
    w_end = dt * jnp.exp(tot - cum)
    cdec = jnp.exp(tot)
    stacked = jnp.concatenate([w_end, ecum, jnp.broadcast_to(cdec, (8, n_heads))], axis=0)
    s_hi, s_lo = _split_bf16(stacked)
    expanded = _dot(s_hi, e_ref[...]) + _dot(s_lo, e_ref[...])
    w_x, ecum_x, cdec_x = expanded[0:L], expanded[L:2 * L], expanded[2 * L:2 * L + 1]
    mask = tri > 0.5
    lane = lax.broadcasted_iota(I32, (L, 2 * head_dim), 1)
    for g in range(groups):
        bg = b_ref[0, :, g * n_state:(g + 1) * n_state]
        cg = c_ref[0, :, g * n_state:(g + 1) * n_state]
        cb = _dot_nt(cg, bg)
        mats = []
        for r in range(heads_per_group):
            h = g * heads_per_group + r
            seg = cum[:, h:h + 1] - cumT[h:h + 1, :]
            dec = jnp.exp(jnp.where(mask, seg, 0.0))
            mats.append((jnp.where(mask, cb * dec, 0.0) * dtT[h:h + 1, :]).astype(BF16))
        pairs = []
        for q in range(heads_per_group // 2):
            c0 = g * gw + q * 2 * head_dim
            xp = x_ref[0, :, c0:c0 + 2 * head_dim]
            pairs.append(jnp.where(lane < head_dim, _dot(mats[2 * q], xp), _dot(mats[2 * q + 1], xp)))
        y_diag = jnp.concatenate(pairs, axis=1)
        cols = slice(g * gw, (g + 1) * gw)
        xg = x_ref[0, :, cols]
        s_in = st_ref[g]
        y_off = _dot(cg, s_in.astype(BF16)) * ecum_x[:, cols]
        xw = (xg.astype(F32) * w_x[:, cols]).astype(BF16)
        st_ref[g] = s_in * cdec_x[:, cols] + _dot_tn(bg, xw)
        y_ref[0, 0, :, cols] = y_diag + y_off


def ssd_scan(xbc, dt, dtT, a_log, *, ctx_chunks, groups, n_heads, head_dim, n_state):
    bsz, s, _ = xbc.shape
    L = SSD_CHUNK
    nc = s // L
    d_inner = n_heads * head_dim
    gn = groups * n_state
    idx = jnp.arange(L)
    tril = (idx[:, None] >= idx[None, :]).astype(F32)
    tri = jnp.stack([tril, tril.T])
    expand = (jnp.arange(n_heads)[:, None] == (jnp.arange(d_inner)[None, :] // head_dim)).astype(BF16)

    def chunk(d, t):
        bwd = jnp.where(t < ctx_chunks, ctx_chunks - 1 - t, nc - 1 - (t - ctx_chunks))
        return jnp.where(d == 0, t, bwd)

    xb, bb, cbk = 0, d_inner // gn, d_inner // gn + 1
    return pl.pallas_call(
        functools.partial(_ssd_scan_kernel, groups=groups, heads_per_group=n_heads // groups,
                          head_dim=head_dim, n_state=n_state),
        grid=(bsz, 2, nc),
        in_specs=[pl.BlockSpec((1, L, d_inner), lambda b, d, t: (b, chunk(d, t), xb)),
                  pl.BlockSpec((1, L, gn), lambda b, d, t: (b, chunk(d, t), bb)),
                  pl.BlockSpec((1, L, gn), lambda b, d, t: (b, chunk(d, t), cbk)),
                  pl.BlockSpec((1, 1, L, n_heads), lambda b, d, t: (b, d, chunk(d, t), 0)),
                  pl.BlockSpec((1, 1, n_heads, L), lambda b, d, t: (b, d, 0, chunk(d, t))),
                  pl.BlockSpec((1, 1, n_heads), lambda b, d, t: (d, 0, 0)),
                  pl.BlockSpec((1, n_heads, 1), lambda b, d, t: (d, 0, 0)),
                  pl.BlockSpec((1, L, L), lambda b, d, t: (d, 0, 0)),
                  pl.BlockSpec((1, L, L), lambda b, d, t: (1 - d, 0, 0)),
                  pl.BlockSpec((n_heads, d_inner), lambda b, d, t: (0, 0))],
        out_specs=pl.BlockSpec((1, 1, L, d_inner), lambda b, d, t: (d, b, chunk(d, t), 0)),
        out_shape=jax.ShapeDtypeStruct((2, bsz, s, d_inner), F32),
        scratch_shapes=[pltpu.VMEM((groups, n_state, d_inner // groups), F32)],
        compiler_params=_cparams(3),
        name="ssd_scan",
    )(xbc, xbc, xbc, dt, dtT, a_log.reshape(2, 1, n_heads), a_log.reshape(2, n_heads, 1), tri, tri, expand)


def _post_mixer(o, x_ref, g1_ref, sh2_ref, sc2_ref, lng_ref, lnb_ref, rw_ref, rb_ref, xn_ref, tok_ref, lg_ref):
    xn = _layer_norm(DEEPNORM_ALPHA * x_ref[0] + g1_ref[0] * o, lng_ref[...], lnb_ref[...])
    xn_ref[0] = xn
    tok = xn * (1.0 + sc2_ref[0]) + sh2_ref[0]
    tok_ref[...] = tok
    lg_ref[...] = jnp.dot(tok, rw_ref[...], precision=HIGHEST, preferred_element_type=F32) + rb_ref[...]


def _ssd_out_kernel(yf_ref, yb_ref, xa_ref, z_ref, dx_ref, nw_ref, w_ref,
                    x_ref, g1_ref, sh2_ref, sc2_ref, lng_ref, lnb_ref, rw_ref, rb_ref,
                    xn_ref, tok_ref, lg_ref, *, groups):
    z = z_ref[0].astype(F32)
    y = (yf_ref[0, 0] + yb_ref[0, 0] + xa_ref[0].astype(F32) * dx_ref[...]) * (z * _sigmoid(z))
    gw = y.shape[1] // groups
    parts = []
    for g in range(groups):
        yg = y[:, g * gw:(g + 1) * gw]
        parts.append(yg * lax.rsqrt(jnp.mean(yg * yg, axis=-1, keepdims=True) + GATED_NORM_EPS))
    yn = (jnp.concatenate(parts, axis=1) * nw_ref[...]).astype(BF16)
    _post_mixer(_dot(yn, w_ref[...]), x_ref, g1_ref, sh2_ref, sc2_ref, lng_ref, lnb_ref, rw_ref, rb_ref,
                xn_ref, tok_ref, lg_ref)


def _post_specs(d, row, nt):
    ins = [_mk_mod_spec(d, 2, row), _mk_mod_spec(d, 3, row), _mk_mod_spec(d, 4, row),
           pl.BlockSpec((1, d), lambda b, j: (0, 0)), pl.BlockSpec((1, d), lambda b, j: (0, 0)),
           pl.BlockSpec((d, LANES), lambda b, j: (0, 0)), pl.BlockSpec((1, LANES), lambda b, j: (0, 0))]
    outs = [pl.BlockSpec((1, ROW_TILE, d), lambda b, j: (b, j, 0)),
            pl.BlockSpec((ROW_TILE, d), lambda b, j: (b * nt + j, 0)),
            pl.BlockSpec((ROW_TILE, LANES), lambda b, j: (b * nt + j, 0))]
    return ins, outs


def ssd_out(y, xbc, u, xs, mods, d_x, norm_w, w_out, lng, lnb, rw, rb, *, ctx_tiles, ctx_row):
    bsz, s, d = xs.shape
    d_inner = y.shape[3]
    tm = ROW_TILE
    nt = s // tm
    row = lambda b, j: jnp.where(j < ctx_tiles, ctx_row, b)
    post_in, post_out = _post_specs(d, row, nt)
    return pl.pallas_call(
        functools.partial(_ssd_out_kernel, groups=SSD_GROUPS),
        grid=(bsz, nt),
        in_specs=[pl.BlockSpec((1, 1, tm, d_inner), lambda b, j: (0, b, j, 0)),
                  pl.BlockSpec((1, 1, tm, d_inner), lambda b, j: (1, b, j, 0)),
                  pl.BlockSpec((1, tm, d_inner), lambda b, j: (b, j, 0)),
                  pl.BlockSpec((1, tm, d_inner), lambda b, j: (b, j, 0)),
                  pl.BlockSpec((1, d_inner), lambda b, j: (0, 0)),
                  pl.BlockSpec((1, d_inner), lambda b, j: (0, 0)),
                  pl.BlockSpec((d_inner, d), lambda b, j: (0, 0)),
                  pl.BlockSpec((1, tm, d), lambda b, j: (b, j, 0))] + post_in,
        out_specs=post_out,
        out_shape=[jax.ShapeDtypeStruct((bsz, s, d), F32),
                   jax.ShapeDtypeStruct((bsz * s, d), F32),
                   jax.ShapeDtypeStruct((bsz * s, LANES), F32)],
        compiler_params=_cparams(2),
        name="ssd_out",
    )(y, y, xbc, u, d_x, norm_w, w_out, xs, mods, mods, mods, lng, lnb, rw, rb)


def _route_kernel(lg_ref, ls_ref, up_ref, slots_ref, gates_ref, cnt_ref, cnt_s, run_s, offs_s, *, n_exp, tm):
    p, j = pl.program_id(0), pl.program_id(1)
    tr = lg_ref.shape[0]
    lane = lax.broadcasted_iota(I32, (tr, LANES), 1).astype(F32)
    lg = jnp.where(lane < n_exp, lg_ref[...], -jnp.inf)
    onehots, tops = [], []
    for _ in range(TOP_K):
        m = jnp.max(lg, axis=-1, keepdims=True)
        idx = jnp.min(jnp.where(lg == m, lane, float(LANES)), axis=-1, keepdims=True)
        oh = lane == idx
        onehots.append(oh)
        tops.append(m)
        lg = jnp.where(oh, -jnp.inf, lg)
    multi = sum(jnp.where(oh, 1.0, 0.0) for oh in onehots)
    colsum = jnp.sum(multi, axis=0, keepdims=True)

    @pl.when(jnp.logical_and(p == 0, j == 0))
    def _():
        cnt_s[...] = jnp.zeros_like(cnt_s)

    @pl.when(p == 0)
    def _():
        cnt_s[...] += colsum

    @pl.when(p == 1)
    def _():
        @pl.when(j == 0)
        def _():
            cnt = cnt_s[...]
            padded = jnp.ceil(cnt * (1.0 / tm)) * tm
            offs = jnp.dot(jnp.broadcast_to(padded, (8, LANES)), up_ref[...], precision=HIGHEST,
                           preferred_element_type=F32)
            offs_s[...] = offs[0:1]
            run_s[...] = jnp.zeros_like(run_s)
            cnt_ref[...] = jnp.broadcast_to(cnt, (8, LANES))

        before = _dot(ls_ref[...], multi.astype(BF16)) + run_s[...]
        pos = before + offs_s[...]
        gates = jnp.zeros((tr, LANES), F32)
        exps = [jnp.exp(t - tops[0]) for t in tops]
        den = sum(exps)
        ones = jnp.ones((8, LANES), F32)
        sub = lax.broadcasted_iota(I32, (8, tr), 0)
        slots = jnp.zeros((8, tr), F32)
        for k in range(TOP_K):
            sk = lax.dot_general(ones, jnp.where(onehots[k], pos, 0.0), (((1,), (1,)), ((), ())),
                                 precision=HIGHEST, preferred_element_type=F32)
            slots = jnp.where(sub == k, sk, slots)
            gates = jnp.where(lane == k, exps[k] / den, gates)
        slots_ref[...] = slots.astype(I32)
        gates_ref[...] = gates
        run_s[...] += colsum


def moe_route(logits):
    t = logits.shape[0]
    tr = ROUTE_TILE
    nt = t // tr
    idx = jnp.arange(tr)
    strict_lower = (idx[:, None] > idx[None, :]).astype(BF16)
    ie = jnp.arange(LANES)
    strict_upper = (ie[:, None] < ie[None, :]).astype(F32)
    return pl.pallas_call(
        functools.partial(_route_kernel, n_exp=N_EXPERTS, tm=MOE_TILE),
        grid=(2, nt),
        in_specs=[pl.BlockSpec((tr, LANES), lambda p, j: (j, 0)),
                  pl.BlockSpec((tr, tr), lambda p, j: (0, 0)),
                  pl.BlockSpec((LANES, LANES), lambda p, j: (0, 0))],
        out_specs=[pl.BlockSpec((8, tr), lambda p, j: (j * p, 0)),
                   pl.BlockSpec((tr, LANES), lambda p, j: (j * p, 0)),
                   pl.BlockSpec((8, LANES), lambda p, j: (0, 0))],
        out_shape=[jax.ShapeDtypeStruct((nt * 8, tr), I32),
                   jax.ShapeDtypeStruct((t, LANES), F32),
                   jax.ShapeDtypeStruct((8, LANES), F32)],
        scratch_shapes=[pltpu.VMEM((1, LANES), F32)] * 3,
        compiler_params=_cparams(2),
        name="moe_route",
    )(logits, strict_lower, strict_upper)


def _row_copy(src_ref, src_row, dst_ref, dst_row, sem):
    return pltpu.make_async_copy(src_ref.at[pl.ds(src_row, 1)], dst_ref.at[pl.ds(dst_row, 1)], sem)


def _dispatch_kernel(pad_start_ref, pad_len_ref, slots_ref, tok_ref, xs_ref, zrow, sem, *, n_exp):
    i = pl.program_id(0)
    td = tok_ref.shape[0]

    @pl.when(i == 0)
    def _():
        zrow[...] = jnp.zeros_like(zrow)
        for start in (True, False):
            def per_expert(e, c, start=start):
                def per_row(r, c2):
                    cp = _row_copy(zrow, 0, xs_ref, pad_start_ref[e] + r, sem)
                    cp.start() if start else cp.wait()
                    return c2
                return lax.fori_loop(0, pad_len_ref[e], per_row, c)
            lax.fori_loop(0, n_exp, per_expert, 0)

    base = (i % (ROUTE_TILE // td)) * td
    for start in (True, False):
        def per_token(t, c, start=start):
            for k in range(TOP_K):
                cp = _row_copy(tok_ref, t, xs_ref, slots_ref[k * ROUTE_TILE + base + t] if start else 0, sem)
                cp.start(priority=k % 2) if start else cp.wait()
            return c
        lax.fori_loop(0, td, per_token, 0, unroll=4)


def moe_dispatch(tok, slots_flat, pad_start, pad_len, n_slots):
    t, d = tok.shape
    td = ROW_TILE
    per = ROUTE_TILE // td
    return pl.pallas_call(
        functools.partial(_dispatch_kernel, n_exp=N_EXPERTS),
        grid_spec=pltpu.PrefetchScalarGridSpec(
            num_scalar_prefetch=2,
            grid=(t // td,),
            in_specs=[pl.BlockSpec((ROUTE_TILE * TOP_K,), lambda i, *_: (i // per,), memory_space=pltpu.SMEM),
                      pl.BlockSpec((td, d), lambda i, *_: (i, 0))],
            out_specs=pl.BlockSpec(memory_space=pl.ANY),
            scratch_shapes=[pltpu.VMEM((8, d), F32), pltpu.SemaphoreType.DMA(())]),
        out_shape=jax.ShapeDtypeStruct((n_slots, d), F32),
        compiler_params=_cparams(1),
        name="moe_dispatch",
    )(pad_start, pad_len, slots_flat, tok)


def _experts_kernel(be_ref, nu_ref, x_ref, w1_ref, b1_ref, w2_ref, b2_ref, o_ref, w1_s, w2_s):
    b = pl.program_id(0)
    prev = be_ref[jnp.maximum(b - 1, 0)]

    @pl.when(b < nu_ref[0])
    def _():
        @pl.when(jnp.logical_or(b == 0, be_ref[b] != prev))
        def _():
            w1_s[...] = w1_ref[0].astype(BF16)
            w2_s[...] = w2_ref[0].astype(BF16)

        u = _dot(x_ref[...].astype(BF16), w1_s[...]) + b1_ref[0]
        f = u.shape[1] // 2
        glu = jnp.minimum(u[:, :f], SWIGLU_LIMIT)
        lin = jnp.clip(u[:, f:], -SWIGLU_LIMIT, SWIGLU_LIMIT)
        act = (glu * _sigmoid(SWIGLU_ALPHA * glu) * (lin + 1.0)).astype(BF16)
        o_ref[...] = _dot(act, w2_s[...]) + b2_ref[0]


def moe_experts(xs, block_e, n_used, w_in, b_in, w_out, b_out):
    n_slots, d = xs.shape
    n_exp, _, f2 = w_in.shape
    tm = MOE_TILE
    blk = lambda b, be, nu: (jnp.minimum(b, nu[0] - 1), 0)
    exp = lambda b, be, nu: (be[b], 0, 0)
    return pl.pallas_call(
        _experts_kernel,
        grid_spec=pltpu.PrefetchScalarGridSpec(
            num_scalar_prefetch=2,
            grid=(n_slots // tm,),
            in_specs=[pl.BlockSpec((tm, d), blk),
                      pl.BlockSpec((1, d, f2), exp),
                      pl.BlockSpec((1, 1, f2), exp),
                      pl.BlockSpec((1, f2 // 2, d), exp),
                      pl.BlockSpec((1, 1, d), exp)],
            out_specs=pl.BlockSpec((tm, d), blk),
            scratch_shapes=[pltpu.VMEM((d, f2), BF16), pltpu.VMEM((f2 // 2, d), BF16)]),
        out_shape=jax.ShapeDtypeStruct((n_slots, d), F32),
        compiler_params=_cparams(1, VMEM_LIMIT_MOE),
        name="moe_experts",
    )(block_e, n_used, xs, w_in, b_in.reshape(n_exp, 1, f2), w_out, b_out.reshape(n_exp, 1, d))


def _combine_kernel(s_cur_ref, s_nxt_ref, gates_ref, x_ref, g2_ref, lng_ref, lnb_ref, ys_ref, o_ref,
                    buf, sem, *, n_tiles):
    i = pl.program_id(0)
    tc = x_ref.shape[1]

    per = ROUTE_TILE // tc

    def gather(slot_ref, tile, s, start):
        base = (tile % per) * tc
        def per_token(t, c):
            for k in range(TOP_K):
                row = slot_ref[k * ROUTE_TILE + base + t] if start else 0
                cp = pltpu.make_async_copy(ys_ref.at[pl.ds(row, 1)], buf.at[s, k, pl.ds(t, 1)], sem.at[s])
                cp.start(priority=k % 2) if start else cp.wait()
            return c
        lax.fori_loop(0, tc, per_token, 0, unroll=4)

    @pl.when(i == 0)
    def _():
        gather(s_cur_ref, i, 0, True)

    @pl.when(i + 1 < n_tiles)
    def _():
        gather(s_nxt_ref, i + 1, (i + 1) % 2, True)

    s = i % 2
    gather(s_cur_ref, i, s, False)
    g = gates_ref[...]
    f = g[:, 0:1] * buf[s, 0]
    for k in range(1, TOP_K):
        f = f + g[:, k:k + 1] * buf[s, k]
    o_ref[0] = _layer_norm(DEEPNORM_ALPHA * x_ref[0] + g2_ref[0] * f, lng_ref[...], lnb_ref[...])


def moe_combine(ys, slots_flat, gates, xn, mods, lng, lnb, *, tiles_per_b, tile0, row_fn):
    bsz, _, d = xn.shape
    tc = ROW_TILE
    n_tiles = bsz * tiles_per_b
    where = lambda i: (i // tiles_per_b, i % tiles_per_b + tile0, 0)
    per = ROUTE_TILE // tc
    return pl.pallas_call(
        functools.partial(_combine_kernel, n_tiles=n_tiles),
        grid=(n_tiles,),
        in_specs=[pl.BlockSpec((ROUTE_TILE * TOP_K,), lambda i: (i // per,), memory_space=pltpu.SMEM),
                  pl.BlockSpec((ROUTE_TILE * TOP_K,), lambda i: (jnp.minimum(i + 1, n_tiles - 1) // per,),
                               memory_space=pltpu.SMEM),
                  pl.BlockSpec((tc, LANES), lambda i: (i, 0)),
                  pl.BlockSpec((1, tc, d), where),
                  pl.BlockSpec((1, 1, d), lambda i: (row_fn(i // tiles_per_b, i % tiles_per_b + tile0) * 6 + 5, 0, 0)),
                  pl.BlockSpec((1, d), lambda i: (0, 0)),
                  pl.BlockSpec((1, d), lambda i: (0, 0)),
                  pl.BlockSpec(memory_space=pl.ANY)],
        out_specs=pl.BlockSpec((1, tc, d), lambda i: (i // tiles_per_b, i % tiles_per_b, 0)),
        out_shape=jax.ShapeDtypeStruct((bsz, tiles_per_b * tc, d), F32),
        scratch_shapes=[pltpu.VMEM((2, TOP_K, tc, d), F32), pltpu.SemaphoreType.DMA((2,))],
        compiler_params=_cparams(1),
        name="moe_combine",
    )(slots_flat, slots_flat, gates, xn, mods, lng, lnb, ys)


def moe_ffn(tok, logits, w_in, b_in, w_out, b_out):
    t = tok.shape[0]
    tm = MOE_TILE
    n_blocks = (t * TOP_K + N_EXPERTS * (tm - 1) + tm - 1) // tm
    slots, gates, cnt = moe_route(logits)
    counts = cnt[0, :N_EXPERTS].astype(I32)
    padded = (counts + tm - 1) // tm * tm
    padded_end = jnp.cumsum(padded)
    pad_start = padded_end - padded + counts
    n_used = (padded_end[-1:] // tm).astype(I32)
    block_e = jnp.minimum(jnp.sum((padded_end[None, :] <= (jnp.arange(n_blocks) * tm)[:, None]).astype(I32), axis=1),
                          N_EXPERTS - 1).astype(I32)
    slots_flat = slots.reshape(t // ROUTE_TILE, 8, ROUTE_TILE)[:, :TOP_K, :].reshape(-1)
    xs = moe_dispatch(tok, slots_flat, pad_start.astype(I32), (padded - counts).astype(I32), n_blocks * tm)
    ys = moe_experts(xs, block_e, n_used, w_in, b_in, w_out, b_out)
    return ys, slots_flat, gates


def _mla_proj_kernel(*refs, with_q):
    if with_q:
        (x_ref, sc_ref, sh_ref, wa_ref, qn_ref, kvn_ref, wq_ref, wkv_ref, cos_ref, sin_ref,
         qn_out, qr_out, kv_out, kr_out) = refs
    else:
        x_ref, sc_ref, sh_ref, wa_ref, kvn_ref, wkv_ref, kv_out, kr_out = refs
    h = (x_ref[0] * (1.0 + sc_ref[0]) + sh_ref[0]).astype(BF16)
    a = _dot(h, wa_ref[...])
    q0, kv0 = 0, MLA_Q_LORA
    kr0 = MLA_Q_LORA + MLA_KV_LORA
    kp0 = kr0 + LANES
    kvn = _rms_norm(a[:, kv0:kv0 + MLA_KV_LORA], kvn_ref[...]).astype(BF16)
    kv_out[0] = _dot(kvn, wkv_ref[...]).astype(BF16)
    k_rope = a[:, kr0:kr0 + MLA_ROPE]
    if not with_q:
        kr_out[0] = k_rope.astype(BF16)
        return
    cos, sin = cos_ref[...], sin_ref[...]
    kr_out[0] = (k_rope * cos + a[:, kp0:kp0 + MLA_ROPE] * sin).astype(BF16)
    qn = _rms_norm(a[:, q0:q0 + MLA_Q_LORA], qn_ref[...]).astype(BF16)
    q = _dot(qn, wq_ref[...])
    n_nope = MLA_HEADS * MLA_NOPE
    qn_out[0] = (q[:, 0:n_nope] * MLA_SCALE).astype(BF16)
    for hd in range(MLA_HEADS):
        r0 = n_nope + hd * LANES
        p0 = n_nope + (MLA_HEADS + hd) * LANES
        qr = q[:, r0:r0 + MLA_ROPE] * cos + q[:, p0:p0 + MLA_ROPE] * sin
        qr_out[0, hd] = (qr * MLA_SCALE).astype(BF16)


def mla_proj(xs, mods, wa, q_norm, kv_norm, wq, wkv, cos, sin, *, tile0, n_rows, row_fn, with_q):
    bsz, _, d = xs.shape
    tm = ROW_TILE
    nt = n_rows // tm
    full = lambda shape: pl.BlockSpec(shape, lambda b, j: (0,) * len(shape))
    row = lambda b, j: row_fn(b, j + tile0)
    in_specs = [pl.BlockSpec((1, tm, d), lambda b, j: (b, j + tile0, 0)),
                _mk_mod_spec(d, 1, row), _mk_mod_spec(d, 0, row), full(wa.shape)]
    args = [xs, mods, mods, wa]
    kv_w = wkv.shape[1]
    kv_specs = [pl.BlockSpec((1, tm, kv_w), lambda b, j: (b, j, 0)),
                pl.BlockSpec((1, tm, MLA_ROPE), lambda b, j: (b, j, 0))]
    kv_shapes = [jax.ShapeDtypeStruct((bsz, n_rows, kv_w), BF16),
                 jax.ShapeDtypeStruct((bsz, n_rows, MLA_ROPE), BF16)]
    if with_q:
        in_specs += [full(q_norm.shape), full(kv_norm.shape), full(wq.shape), full(wkv.shape),
                     pl.BlockSpec((tm, MLA_ROPE), lambda b, j: (j, 0)),
                     pl.BlockSpec((tm, MLA_ROPE), lambda b, j: (j, 0))]
        args += [q_norm, kv_norm, wq, wkv, cos, sin]
        n_nope = MLA_HEADS * MLA_NOPE
        out_specs = [pl.BlockSpec((1, tm, n_nope), lambda b, j: (b, j, 0)),
                     pl.BlockSpec((1, MLA_HEADS, tm, MLA_ROPE), lambda b, j: (b, 0, j, 0))] + kv_specs
        out_shape = [jax.ShapeDtypeStruct((bsz, n_rows, n_nope), BF16),
                     jax.ShapeDtypeStruct((bsz, MLA_HEADS, n_rows, MLA_ROPE), BF16)] + kv_shapes
    else:
        in_specs += [full(kv_norm.shape), full(wkv.shape)]
        args += [kv_norm, wkv]
        out_specs, out_shape = kv_specs, kv_shapes
    return pl.pallas_call(
        functools.partial(_mla_proj_kernel, with_q=with_q),
        grid=(bsz, nt),
        in_specs=in_specs, out_specs=out_specs, out_shape=out_shape,
        compiler_params=_cparams(2),
        name="mla_proj_q" if with_q else "mla_proj_ctx",
    )(*args)


def _attn_kernel(qn_ref, qr_ref, knc_ref, knl_ref, vc_ref, vl_ref, krc_ref, krl_ref, o_ref, kcat, qcat,
                 *, n_ctx, kc):
    j = pl.program_id(2)
    n_lat = knl_ref.shape[1]
    rope_end = MLA_NOPE + MLA_ROPE

    @pl.when(j == 0)
    def _():
        kcat[0:n_ctx, 0:MLA_NOPE] = knc_ref[0]
        kcat[n_ctx:n_ctx + n_lat, 0:MLA_NOPE] = knl_ref[0]
        kcat[0:n_ctx, MLA_NOPE:rope_end] = krc_ref[0]
        kcat[n_ctx:n_ctx + n_lat, MLA_NOPE:rope_end] = krl_ref[0]
        kcat[:, rope_end:] = jnp.zeros((kcat.shape[0], kcat.shape[1] - rope_end), BF16)

    tq = qn_ref.shape[1]
    qcat[:, 0:MLA_NOPE] = qn_ref[0]
    qcat[:, MLA_NOPE:rope_end] = qr_ref[0, 0]
    qcat[:, rope_end:] = jnp.zeros((tq, qcat.shape[1] - rope_end), BF16)
    q = qcat[...]
    m = jnp.full((tq, 1), -jnp.inf, F32)
    l = jnp.zeros((tq, 1), F32)
    acc = jnp.zeros((tq, MLA_V), F32)
    chunks = [(0, n_ctx, vc_ref, 0)] + [(n_ctx + c0, kc, vl_ref, c0) for c0 in range(0, n_lat, kc)]
    for k0, kn, v_ref, v0 in chunks:
        s = _dot_nt(q, kcat[k0:k0 + kn])
        m_new = jnp.maximum(m, jnp.max(s, axis=-1, keepdims=True))
        alpha = jnp.exp(m - m_new)
        p = jnp.exp(s - m_new)
        l = alpha * l + jnp.sum(p, axis=-1, keepdims=True)
        acc = alpha * acc + _dot(p.astype(BF16), v_ref[0, v0:v0 + kn])
        m = m_new
    o_ref[0] = (acc / l).astype(BF16)


def mla_attention(qn, qr, kv_c, kv_l, kr_c, kr_l):
    bsz, n_lat, _ = qn.shape
    n_ctx = kv_c.shape[1]
    tq = ATTN_Q_TILE
    kc = min(ATTN_K_CHUNK, n_lat)
    cat_w = 2 * LANES
    return pl.pallas_call(
        functools.partial(_attn_kernel, n_ctx=n_ctx, kc=kc),
        grid=(bsz, MLA_HEADS, n_lat // tq),
        in_specs=[pl.BlockSpec((1, tq, MLA_NOPE), lambda b, h, j: (b, j, h)),
                  pl.BlockSpec((1, 1, tq, MLA_ROPE), lambda b, h, j: (b, h, j, 0)),
                  pl.BlockSpec((1, n_ctx, MLA_NOPE), lambda b, h, j: (b, 0, 2 * h)),
                  pl.BlockSpec((1, n_lat, MLA_NOPE), lambda b, h, j: (b, 0, 2 * h)),
                  pl.BlockSpec((1, n_ctx, MLA_V), lambda b, h, j: (b, 0, 2 * h + 1)),
                  pl.BlockSpec((1, n_lat, MLA_V), lambda b, h, j: (b, 0, 2 * h + 1)),
                  pl.BlockSpec((1, n_ctx, MLA_ROPE), lambda b, h, j: (b, 0, 0)),
                  pl.BlockSpec((1, n_lat, MLA_ROPE), lambda b, h, j: (b, 0, 0))],
        out_specs=pl.BlockSpec((1, tq, MLA_V), lambda b, h, j: (b, j, h)),
        out_shape=jax.ShapeDtypeStruct((bsz, n_lat, MLA_HEADS * MLA_V), BF16),
        scratch_shapes=[pltpu.VMEM((n_ctx + n_lat, cat_w), BF16), pltpu.VMEM((tq, cat_w), BF16)],
        compiler_params=_cparams(3),
        name="mla_attention",
    )(qn, qr, kv_c, kv_l, kv_c, kv_l, kr_c, kr_l)


def _mla_out_kernel(o_ref, w_ref, x_ref, g1_ref, sh2_ref, sc2_ref, lng_ref, lnb_ref, rw_ref, rb_ref,
                    xn_ref, tok_ref, lg_ref):
    _post_mixer(_dot(o_ref[0], w_ref[...]), x_ref, g1_ref, sh2_ref, sc2_ref, lng_ref, lnb_ref, rw_ref, rb_ref,
                xn_ref, tok_ref, lg_ref)


def mla_out(o, w_o, xs, mods, lng, lnb, rw, rb, *, tile0, row_fn):
    bsz, n_rows, dv = o.shape
    d = xs.shape[2]
    tm = ROW_TILE
    nt = n_rows // tm
    row = lambda b, j: row_fn(b, j + tile0)
    post_in, post_out = _post_specs(d, row, nt)
    return pl.pallas_call(
        _mla_out_kernel,
        grid=(bsz, nt),
        in_specs=[pl.BlockSpec((1, tm, dv), lambda b, j: (b, j, 0)),
                  pl.BlockSpec((dv, d), lambda b, j: (0, 0)),
                  pl.BlockSpec((1, tm, d), lambda b, j: (b, j + tile0, 0))] + post_in,
        out_specs=post_out,
        out_shape=[jax.ShapeDtypeStruct((bsz, n_rows, d), F32),
                   jax.ShapeDtypeStruct((bsz * n_rows, d), F32),
                   jax.ShapeDtypeStruct((bsz * n_rows, LANES), F32)],
        compiler_params=_cparams(2),
        name="mla_out",
    )(o, w_o, xs, mods, mods, mods, lng, lnb, rw, rb)


def _rope_tables(seq_len):
    t = jnp.arange(seq_len)
    row = (t // GRID_W).astype(F32)
    col = (t % GRID_W).astype(F32)
    inv = ROPE_BASE ** (-jnp.arange(ROPE_FREQS, dtype=F32) / ROPE_FREQS)
    ang = jnp.stack([row[:, None] * inv, col[:, None] * inv], axis=1)
    cos = jnp.broadcast_to(jnp.cos(ang)[:, :, None, :], (seq_len, 2, 2, ROPE_FREQS)).reshape(seq_len, MLA_ROPE)
    sign = jnp.array([-1.0, 1.0], F32)[None, None, :, None]
    sin = (jnp.sin(ang)[:, :, None, :] * sign).reshape(seq_len, MLA_ROPE)
    return cos, sin


def _rope_partner_perm():
    j = jnp.arange(MLA_ROPE)
    half = (j // ROPE_FREQS) % 2
    return jnp.where(half == 0, j + ROPE_FREQS, j - ROPE_FREQS)


def _pad_cols(w, n):
    return jnp.pad(w, ((0, 0), (0, n - w.shape[1])))


def _router_pad(rw, rb):
    return _pad_cols(rw, LANES), _pad_cols(rb[None, :], LANES)


def kernel(x, c, ctx, c_ctx, mod_w, mod_b, ln1_g, ln1_b, ln2_g, ln2_b, ssd_w_in, ssd_conv_w, ssd_conv_b, ssd_dt_bias, ssd_a_log, ssd_d, ssd_norm_w, ssd_w_out, mla_w_a, mla_q_norm, mla_kv_norm, mla_w_qb, mla_w_kvb, mla_w_o, router_w, router_b, moe_w_in, moe_b_in, moe_w_out, moe_b_out):
    bsz, n_lat, d = x.shape
    n_ctx = ctx.shape[1]
    assert mod_w.shape[0] == DEPTH == 2 and bsz + 1 <= MOD_ROWS
    assert n_ctx % ROW_TILE == 0 and n_lat % ROUTE_TILE == 0 and (bsz * n_ctx) % ROUTE_TILE == 0
    ctx_tiles = n_ctx // ROW_TILE
    lat_tiles = n_lat // ROW_TILE
    ctx_row = bsz
    row_fn = lambda b, j: jnp.where(j < ctx_tiles, ctx_row, b)

    cond = jnp.concatenate([c, c_ctx[None, :], jnp.zeros((MOD_ROWS - bsz - 1, d), F32)], axis=0)
    mods = adaln_mods(cond, mod_w, mod_b).reshape(DEPTH, MOD_ROWS * 6, 1, d)
    xs = jnp.concatenate([ctx, x], axis=1)

    n_heads = ssd_a_log.shape[2]
    d_inner = n_heads * SSD_HEAD_DIM
    conv_dim = ssd_conv_w.shape[2]
    w_in = ssd_w_in[0]
    w_zx = w_in[:, :d_inner + conv_dim].astype(BF16)
    w_dt = w_in[:, d_inner + conv_dim:]
    u, dt, dtT = ssd_in_proj(xs, mods[0], w_zx, _pad_cols(w_dt, LANES).astype(BF16), w_dt.T.astype(BF16),
                             _pad_cols(ssd_dt_bias[0].reshape(1, -1), LANES), ssd_dt_bias[0].reshape(-1, 1),
                             ctx_tiles=ctx_tiles, ctx_row=ctx_row, n_heads=n_heads)
    xbc = ssd_conv(u, jnp.pad(ssd_conv_w[0], ((0, 8 - SSD_CONV), (0, 0))), ssd_conv_b[0][None, :],
                   col0=d_inner, ctx_tiles=ctx_tiles)
    y = ssd_scan(xbc, dt, dtT, ssd_a_log[0], ctx_chunks=n_ctx // SSD_CHUNK, groups=SSD_GROUPS,
                 n_heads=n_heads, head_dim=SSD_HEAD_DIM, n_state=SSD_STATE)
    rw0, rb0 = _router_pad(router_w[0], router_b[0])
    xn, tok, logits = ssd_out(y, xbc, u, xs, mods[0], jnp.repeat(ssd_d[0], SSD_HEAD_DIM)[None, :],
                              ssd_norm_w[0][None, :], ssd_w_out[0].astype(BF16),
                              ln1_g[0][None, :], ln1_b[0][None, :], rw0, rb0, ctx_tiles=ctx_tiles, ctx_row=ctx_row)
    ys, slots, gates = moe_ffn(tok, logits, moe_w_in[0], moe_b_in[0], moe_w_out[0], moe_b_out[0])
    xs = moe_combine(ys, slots, gates, xn, mods[0], ln2_g[0][None, :], ln2_b[0][None, :],
                     tiles_per_b=ctx_tiles + lat_tiles, tile0=0, row_fn=row_fn)

    perm = _rope_partner_perm()
    wa = mla_w_a[0]
    kr0 = MLA_Q_LORA + MLA_KV_LORA
    w_kr = wa[:, kr0:kr0 + MLA_ROPE]
    wa_ext = jnp.concatenate([wa[:, :kr0], _pad_cols(w_kr, LANES), _pad_cols(w_kr[:, perm], LANES)], axis=1).astype(BF16)
    wq = mla_w_qb[0].reshape(MLA_Q_LORA, MLA_HEADS, MLA_NOPE + MLA_ROPE)
    wq_rope = wq[:, :, MLA_NOPE:]
    pad_heads = lambda w: jnp.pad(w, ((0, 0), (0, 0), (0, LANES - MLA_ROPE))).reshape(MLA_Q_LORA, MLA_HEADS * LANES)
    wq_ext = jnp.concatenate([wq[:, :, :MLA_NOPE].reshape(MLA_Q_LORA, MLA_HEADS * MLA_NOPE),
                              pad_heads(wq_rope), pad_heads(wq_rope[:, :, perm])], axis=1).astype(BF16)
    wkv = mla_w_kvb[0].astype(BF16)
    cos, sin = _rope_tables(n_lat)
    q_norm, kv_norm = mla_q_norm[0][None, :], mla_kv_norm[0][None, :]
    qn, qr, kv_l, kr_l = mla_proj(xs, mods[1], wa_ext, q_norm, kv_norm, wq_ext, wkv, cos, sin,
                                  tile0=ctx_tiles, n_rows=n_lat, row_fn=row_fn, with_q=True)
    kv_c, kr_c = mla_proj(xs, mods[1], wa_ext, q_norm, kv_norm, wq_ext, wkv, cos, sin,
                          tile0=0, n_rows=n_ctx, row_fn=row_fn, with_q=False)
    o = mla_attention(qn, qr, kv_c, kv_l, kr_c, kr_l)
    rw1, rb1 = _router_pad(router_w[1], router_b[1])
    xn, tok, logits = mla_out(o, mla_w_o[0].astype(BF16), xs, mods[1], ln1_g[1][None, :], ln1_b[1][None, :],
                              rw1, rb1, tile0=ctx_tiles, row_fn=row_fn)
    ys, slots, gates = moe_ffn(tok, logits, moe_w_in[1], moe_b_in[1], moe_w_out[1], moe_b_out[1])
    lat_row = lambda b, j: b
    return moe_combine(ys, slots, gates, xn, mods[1], ln2_g[1][None, :], ln2_b[1][None, :],
                       tiles_per_b=lat_tiles, tile0=0, row_fn=lat_row)
```

```python
import functools
import math

import jax
import jax.numpy as jnp
from jax import lax
from jax.experimental import pallas as pl
from jax.experimental.pallas import tpu as pltpu

F32, BF16, I32 = jnp.float32, jnp.bfloat16, jnp.int32
HIGHEST = lax.Precision.HIGHEST

DEPTH = 2
GRID_W = 64
DEEPNORM_ALPHA = (2 * DEPTH) ** 0.25
LN_EPS = 1e-5
RMS_EPS = 1e-6
GATED_NORM_EPS = 1e-5
SSD_HEAD_DIM = 64
SSD_GROUPS = 8
SSD_STATE = 128
SSD_CONV = 5
SSD_CHUNK = 128
MLA_HEADS = 8
MLA_NOPE = 128
MLA_ROPE = 64
MLA_V = 128
MLA_Q_LORA = 384
MLA_KV_LORA = 256
MLA_SCALE = (MLA_NOPE + MLA_ROPE) ** -0.5
ROPE_FREQS = MLA_ROPE // 4
ROPE_BASE = 10000.0
N_EXPERTS = 32
TOP_K = 4
SWIGLU_LIMIT = 7.0
SWIGLU_ALPHA = 1.702

LANES = 128
ROW_TILE = 256
MOE_TILE = 256
ROUTE_TILE = 512
ATTN_Q_TILE = 512
ATTN_K_CHUNK = 512
HALO = 16
MOD_ROWS = 8
VMEM_LIMIT = 48 * 1024 * 1024
VMEM_LIMIT_MOE = 56 * 1024 * 1024


def _cparams(n_axes, vmem=VMEM_LIMIT):
    return pltpu.CompilerParams(dimension_semantics=("arbitrary",) * n_axes, vmem_limit_bytes=vmem)


def _sigmoid(v):
    return 1.0 / (1.0 + jnp.exp(-v))


def _softplus(v):
    return jnp.maximum(v, 0.0) + jnp.log1p(jnp.exp(-jnp.abs(v)))


def _layer_norm(v, g, b):
    mu = jnp.mean(v, axis=-1, keepdims=True)
    d = v - mu
    var = jnp.mean(d * d, axis=-1, keepdims=True)
    return d * lax.rsqrt(var + LN_EPS) * g + b


def _rms_norm(v, g):
    return v * lax.rsqrt(jnp.mean(v * v, axis=-1, keepdims=True) + RMS_EPS) * g


def _dot(a, b):
    return jnp.dot(a, b, preferred_element_type=F32)


def _dot_nt(a, b):
    return lax.dot_general(a, b, (((1,), (1,)), ((), ())), preferred_element_type=F32)


def _dot_tn(a, b):
    return lax.dot_general(a, b, (((0,), (0,)), ((), ())), preferred_element_type=F32)


def _split_bf16(v):
    hi = v.astype(BF16)
    lo = (v - hi.astype(F32)).astype(BF16)
    return hi, lo


def _mod_kernel(c_ref, w_ref, b_ref, o_ref):
    c = c_ref[...]
    s = (c * _sigmoid(c)).astype(BF16)
    o_ref[0] = _dot(s, w_ref[0].astype(BF16)) + b_ref[0]


def adaln_mods(cond, mod_w, mod_b):
    depth, d, n = mod_w.shape
    tn = n // 4
    return pl.pallas_call(
        _mod_kernel,
        grid=(depth, n // tn),
        in_specs=[pl.BlockSpec((MOD_ROWS, d), lambda l, j: (0, 0)),
                  pl.BlockSpec((1, d, tn), lambda l, j: (l, 0, j)),
                  pl.BlockSpec((1, 1, tn), lambda l, j: (l, 0, j))],
        out_specs=pl.BlockSpec((1, MOD_ROWS, tn), lambda l, j: (l, 0, j)),
        out_shape=jax.ShapeDtypeStruct((depth, MOD_ROWS, n), F32),
        compiler_params=_cparams(2),
        name="adaln_mods",
    )(cond, mod_w, mod_b.reshape(depth, 1, n))


def _mk_mod_spec(d, which, row_fn):
    return pl.BlockSpec((1, 1, d), lambda *idx: (row_fn(*idx) * 6 + which, 0, 0))


def _ssd_in_kernel(x_ref, sc_ref, sh_ref, w_ref, wdt_ref, wdtT_ref, bias_ref, biasT_ref,
                   u_ref, dt_ref, dtT_ref, *, n_heads):
    h = (x_ref[0] * (1.0 + sc_ref[0]) + sh_ref[0]).astype(BF16)
    n_out = u_ref.shape[2]
    step = 512
    for n in range(n_out // step):
        u_ref[0, :, n * step:(n + 1) * step] = _dot(h, w_ref[:, n * step:(n + 1) * step]).astype(BF16)
    dt = _softplus(_dot(h, wdt_ref[...]) + bias_ref[...])
    dt_ref[0, 0] = dt[:, 0:n_heads]
    dt_ref[0, 1] = dt[:, n_heads:2 * n_heads]
    dtT = _softplus(_dot_nt(wdtT_ref[...], h) + biasT_ref[...])
    dtT_ref[0, 0] = dtT[0:n_heads]
    dtT_ref[0, 1] = dtT[n_heads:2 * n_heads]


def ssd_in_proj(xs, mods, w_zx, w_dt, w_dtT, dt_bias, dt_biasT, *, ctx_tiles, ctx_row, n_heads):
    bsz, s, d = xs.shape
    n_out = w_zx.shape[1]
    tm = ROW_TILE
    row = lambda b, j: jnp.where(j < ctx_tiles, ctx_row, b)
    return pl.pallas_call(
        functools.partial(_ssd_in_kernel, n_heads=n_heads),
        grid=(bsz, s // tm),
        in_specs=[pl.BlockSpec((1, tm, d), lambda b, j: (b, j, 0)),
                  _mk_mod_spec(d, 1, row), _mk_mod_spec(d, 0, row),
                  pl.BlockSpec((d, n_out), lambda b, j: (0, 0)),
                  pl.BlockSpec((d, LANES), lambda b, j: (0, 0)),
                  pl.BlockSpec((2 * n_heads, d), lambda b, j: (0, 0)),
                  pl.BlockSpec((1, LANES), lambda b, j: (0, 0)),
                  pl.BlockSpec((2 * n_heads, 1), lambda b, j: (0, 0))],
        out_specs=[pl.BlockSpec((1, tm, n_out), lambda b, j: (b, j, 0)),
                   pl.BlockSpec((1, 2, tm, n_heads), lambda b, j: (b, 0, j, 0)),
                   pl.BlockSpec((1, 2, n_heads, tm), lambda b, j: (b, 0, 0, j))],
        out_shape=[jax.ShapeDtypeStruct((bsz, s, n_out), BF16),
                   jax.ShapeDtypeStruct((bsz, 2, s, n_heads), F32),
                   jax.ShapeDtypeStruct((bsz, 2, n_heads, s), F32)],
        compiler_params=_cparams(2),
        name="ssd_in_proj",
    )(xs, mods, mods, w_zx, w_dt, w_dtT, dt_bias, dt_biasT)


def _ssd_conv_kernel(prev_ref, cur_ref, next_ref, w_ref, b_ref, o_ref, scr, *, tc, ctx_tiles, n_tiles):
    j = pl.program_id(1)
    has_prev = jnp.logical_and(j != 0, j != ctx_tiles)
    has_next = jnp.logical_and(j != ctx_tiles - 1, j != n_tiles - 1)
    scr[0:HALO] = jnp.where(has_prev, prev_ref[0].astype(F32), 0.0)
    scr[HALO:HALO + tc] = cur_ref[0].astype(F32)
    scr[HALO + tc:2 * HALO + tc] = jnp.where(has_next, next_ref[0].astype(F32), 0.0)
    pad = SSD_CONV // 2
    acc = b_ref[...] + w_ref[0:1] * scr[HALO - pad:HALO - pad + tc]
    for k in range(1, SSD_CONV):
        acc = acc + w_ref[k:k + 1] * scr[HALO - pad + k:HALO - pad + k + tc]
    o_ref[0] = (acc * _sigmoid(acc)).astype(BF16)


def ssd_conv(u, conv_w, conv_b, *, col0, ctx_tiles):
    bsz, s, _ = u.shape
    c = conv_w.shape[1]
    tc, wc = ROW_TILE, 2048
    n_tiles = s // tc
    cb0 = col0 // wc
    hb = tc // HALO
    return pl.pallas_call(
        functools.partial(_ssd_conv_kernel, tc=tc, ctx_tiles=ctx_tiles, n_tiles=n_tiles),
        grid=(bsz, n_tiles, c // wc),
        in_specs=[pl.BlockSpec((1, HALO, wc), lambda b, j, k: (b, jnp.maximum(j * hb - 1, 0), cb0 + k)),
                  pl.BlockSpec((1, tc, wc), lambda b, j, k: (b, j, cb0 + k)),
                  pl.BlockSpec((1, HALO, wc), lambda b, j, k: (b, jnp.minimum((j + 1) * hb, s // HALO - 1), cb0 + k)),
                  pl.BlockSpec((8, wc), lambda b, j, k: (0, k)),
                  pl.BlockSpec((1, wc), lambda b, j, k: (0, k))],
        out_specs=pl.BlockSpec((1, tc, wc), lambda b, j, k: (b, j, k)),
        out_shape=jax.ShapeDtypeStruct((bsz, s, c), BF16),
        scratch_shapes=[pltpu.VMEM((tc + 2 * HALO, wc), F32)],
        compiler_params=_cparams(3),
        name="ssd_conv",
    )(u, u, u, conv_w, conv_b)


def _ssd_scan_kernel(x_ref, b_ref, c_ref, dt_ref, dtT_ref, alog_ref, alogT_ref, tri_ref, triT_ref, e_ref,
                     y_ref, st_ref, *, groups, heads_per_group, head_dim, n_state):
    step = pl.program_id(2)

    @pl.when(step == 0)
    def _():
        st_ref[...] = jnp.zeros_like(st_ref)

    L = x_ref.shape[1]
    n_heads = groups * heads_per_group
    gw = heads_per_group * head_dim
    dt = dt_ref[0, 0]
    dtT = dtT_ref[0, 0]
    la = dt * (-jnp.exp(alog_ref[0]))
    laT = dtT * (-jnp.exp(alogT_ref[0]))
    tri = tri_ref[0]
    cum = jnp.dot(tri, la, precision=HIGHEST, preferred_element_type=F32)
    cumT = jnp.dot(laT, triT_ref[0], precision=HIGHEST, preferred_element_type=F32)
    tot = jnp.sum(la, axis=0, keepdims=True)
    ecum = jnp.exp(cum)
    w_end = dt * jnp.exp(tot - cum)
    cdec = jnp.exp(tot)
    stacked = jnp.concatenate([w_end, ecum, jnp.broadcast_to(cdec, (8, n_heads))], axis=0)
    s_hi, s_lo = _split_bf16(stacked)
    expanded = _dot(s_hi, e_ref[...]) + _dot(s_lo, e_ref[...])
    w_x, ecum_x, cdec_x = expanded[0:L], expanded[L:2 * L], expanded[2 * L:2 * L + 1]
    mask = tri > 0.5
    lane = lax.broadcasted_iota(I32, (L, 2 * head_dim), 1)
    for g in range(groups):
        bg = b_ref[0, :, g * n_state:(g + 1) * n_state]
        cg = c_ref[0, :, g * n_state:(g + 1) * n_state]
        cb = _dot_nt(cg, bg)
        mats = []
        for r in range(heads_per_group):
            h = g * heads_per_group + r
            seg = cum[:, h:h + 1] - cumT[h:h + 1, :]
            dec = jnp.exp(jnp.where(mask, seg, 0.0))
            mats.append((jnp.where(mask, cb * dec, 0.0) * dtT[h:h + 1, :]).astype(BF16))
        pairs = []
        for q in range(heads_per_group // 2):
            c0 = g * gw + q * 2 * head_dim
            xp = x_ref[0, :, c0:c0 + 2 * head_dim]
            pairs.append(jnp.where(lane < head_dim, _dot(mats[2 * q], xp), _dot(mats[2 * q + 1], xp)))
        y_diag = jnp.concatenate(pairs, axis=1)
        cols = slice(g * gw, (g + 1) * gw)
        xg = x_ref[0, :, cols]
        s_in = st_ref[g]
        y_off = _dot(cg, s_in.astype(BF16)) * ecum_x[:, cols]
        xw = (xg.astype(F32) * w_x[:, cols]).astype(BF16)
        st_ref[g] = s_in * cdec_x[:, cols] + _dot_tn(bg, xw)
        y_ref[0, 0, :, cols] = y_diag + y_off


def ssd_scan(xbc, dt, dtT, a_log, *, ctx_chunks, groups, n_heads, head_dim, n_state):
    bsz, s, _ = xbc.shape
    L = SSD_CHUNK
    nc = s // L
    d_inner = n_heads * head_dim
    gn = groups * n_state
    idx = jnp.arange(L)
    tril = (idx[:, None] >= idx[None, :]).astype(F32)
    tri = jnp.stack([tril, tril.T])
    expand = (jnp.arange(n_heads)[:, None] == (jnp.arange(d_inner)[None, :] // head_dim)).astype(BF16)

    def chunk(d, t):
        bwd = jnp.where(t < ctx_chunks, ctx_chunks - 1 - t, nc - 1 - (t - ctx_chunks))
        return jnp.where(d == 0, t, bwd)

    xb, bb, cbk = 0, d_inner // gn, d_inner // gn + 1
    return pl.pallas_call(
        functools.partial(_ssd_scan_kernel, groups=groups, heads_per_group=n_heads // groups,
                          head_dim=head_dim, n_state=n_state),
        grid=(bsz, 2, nc),
        in_specs=[pl.BlockSpec((1, L, d_inner), lambda b, d, t: (b, chunk(d, t), xb)),
                  pl.BlockSpec((1, L, gn), lambda b, d, t: (b, chunk(d, t), bb)),
                  pl.BlockSpec((1, L, gn), lambda b, d, t: (b, chunk(d, t), cbk)),
                  pl.BlockSpec((1, 1, L, n_heads), lambda b, d, t: (b, d, chunk(d, t), 0)),
                  pl.BlockSpec((1, 1, n_heads, L), lambda b, d, t: (b, d, 0, chunk(d, t))),
                  pl.BlockSpec((1, 1, n_heads), lambda b, d, t: (d, 0, 0)),
                  pl.BlockSpec((1, n_heads, 1), lambda b, d, t: (d, 0, 0)),
                  pl.BlockSpec((1, L, L), lambda b, d, t: (d, 0, 0)),
                  pl.BlockSpec((1, L, L), lambda b, d, t: (1 - d, 0, 0)),
                  pl.BlockSpec((n_heads, d_inner), lambda b, d, t: (0, 0))],
        out_specs=pl.BlockSpec((1, 1, L, d_inner), lambda b, d, t: (d, b, chunk(d, t), 0)),
        out_shape=jax.ShapeDtypeStruct((2, bsz, s, d_inner), F32),
        scratch_shapes=[pltpu.VMEM((groups, n_state, d_inner // groups), F32)],
        compiler_params=_cparams(3),
        name="ssd_scan",
    )(xbc, xbc, xbc, dt, dtT, a_log.reshape(2, 1, n_heads), a_log.reshape(2, n_heads, 1), tri, tri, expand)


def _post_mixer(o, x_ref, g1_ref, sh2_ref, sc2_ref, lng_ref, lnb_ref, rw_ref, rb_ref, xn_ref, tok_ref, lg_ref):
    xn = _layer_norm(DEEPNORM_ALPHA * x_ref[0] + g1_ref[0] * o, lng_ref[...], lnb_ref[...])
    xn_ref[0] = xn
    tok = xn * (1.0 + sc2_ref[0]) + sh2_ref[0]
    tok_ref[...] = tok
    t_hi, t_lo = _split_bf16(tok)
    w_hi, w_lo = _split_bf16(rw_ref[...])
    lg_ref[...] = _dot(t_hi, w_hi) + _dot(t_lo, w_hi) + _dot(t_hi, w_lo) + rb_ref[...]


def _ssd_out_kernel(yf_ref, yb_ref, xa_ref, z_ref, dx_ref, nw_ref, w_ref,
                    x_ref, g1_ref, sh2_ref, sc2_ref, lng_ref, lnb_ref, rw_ref, rb_ref,
                    xn_ref, tok_ref, lg_ref, *, groups):
    z = z_ref[0].astype(F32)
    y = (yf_ref[0, 0] + yb_ref[0, 0] + xa_ref[0].astype(F32) * dx_ref[...]) * (z * _sigmoid(z))
    gw = y.shape[1] // groups
    parts = []
    for g in range(groups):
        yg = y[:, g * gw:(g + 1) * gw]
        parts.append(yg * lax.rsqrt(jnp.mean(yg * yg, axis=-1, keepdims=True) + GATED_NORM_EPS))
    yn = (jnp.concatenate(parts, axis=1) * nw_ref[...]).astype(BF16)
    _post_mixer(_dot(yn, w_ref[...]), x_ref, g1_ref, sh2_ref, sc2_ref, lng_ref, lnb_ref, rw_ref, rb_ref,
                xn_ref, tok_ref, lg_ref)


def _post_specs(d, row, nt):
    ins = [_mk_mod_spec(d, 2, row), _mk_mod_spec(d, 3, row), _mk_mod_spec(d, 4, row),
           pl.BlockSpec((1, d), lambda b, j: (0, 0)), pl.BlockSpec((1, d), lambda b, j: (0, 0)),
           pl.BlockSpec((d, LANES), lambda b, j: (0, 0)), pl.BlockSpec((1, LANES), lambda b, j: (0, 0))]
    outs = [pl.BlockSpec((1, ROW_TILE, d), lambda b, j: (b, j, 0)),
            pl.BlockSpec((ROW_TILE, d), lambda b, j: (b * nt + j, 0)),
            pl.BlockSpec((ROW_TILE, LANES), lambda b, j: (b * nt + j, 0))]
    return ins, outs


def ssd_out(y, xbc, u, xs, mods, d_x, norm_w, w_out, lng, lnb, rw, rb, *, ctx_tiles, ctx_row):
    bsz, s, d = xs.shape
    d_inner = y.shape[3]
    tm = ROW_TILE
    nt = s // tm
    row = lambda b, j: jnp.where(j < ctx_tiles, ctx_row, b)
    post_in, post_out = _post_specs(d, row, nt)
    return pl.pallas_call(
        functools.partial(_ssd_out_kernel, groups=SSD_GROUPS),
        grid=(bsz, nt),
        in_specs=[pl.BlockSpec((1, 1, tm, d_inner), lambda b, j: (0, b, j, 0)),
                  pl.BlockSpec((1, 1, tm, d_inner), lambda b, j: (1, b, j, 0)),
                  pl.BlockSpec((1, tm, d_inner), lambda b, j: (b, j, 0)),
                  pl.BlockSpec((1, tm, d_inner), lambda b, j: (b, j, 0)),
                  pl.BlockSpec((1, d_inner), lambda b, j: (0, 0)),
                  pl.BlockSpec((1, d_inner), lambda b, j: (0, 0)),
                  pl.BlockSpec((d_inner, d), lambda b, j: (0, 0)),
                  pl.BlockSpec((1, tm, d), lambda b, j: (b, j, 0))] + post_in,
        out_specs=post_out,
        out_shape=[jax.ShapeDtypeStruct((bsz, s, d), F32),
                   jax.ShapeDtypeStruct((bsz * s, d), F32),
                   jax.ShapeDtypeStruct((bsz * s, LANES), F32)],
        compiler_params=_cparams(2),
        name="ssd_out",
    )(y, y, xbc, u, d_x, norm_w, w_out, xs, mods, mods, mods, lng, lnb, rw, rb)


def _route_kernel(lg_ref, ls_ref, up_ref, slots_ref, gates_ref, cnt_ref, cnt_s, run_s, offs_s, *, n_exp, tm):
    p, j = pl.program_id(0), pl.program_id(1)
    tr = lg_ref.shape[0]
    lane = lax.broadcasted_iota(I32, (tr, LANES), 1).astype(F32)
    lg = jnp.where(lane < n_exp, lg_ref[...], -jnp.inf)
    onehots, tops = [], []
    for _ in range(TOP_K):
        m = jnp.max(lg, axis=-1, keepdims=True)
        idx = jnp.min(jnp.where(lg == m, lane, float(LANES)), axis=-1, keepdims=True)
        oh = lane == idx
        onehots.append(oh)
        tops.append(m)
        lg = jnp.where(oh, -jnp.inf, lg)
    multi = sum(jnp.where(oh, 1.0, 0.0) for oh in onehots)
    colsum = jnp.sum(multi, axis=0, keepdims=True)

    @pl.when(jnp.logical_and(p == 0, j == 0))
    def _():
        cnt_s[...] = jnp.zeros_like(cnt_s)

    @pl.when(p == 0)
    def _():
        cnt_s[...] += colsum

    @pl.when(p == 1)
    def _():
        @pl.when(j == 0)
        def _():
            cnt = cnt_s[...]
            padded = jnp.ceil(cnt * (1.0 / tm)) * tm
            offs = jnp.dot(jnp.broadcast_to(padded, (8, LANES)), up_ref[...], precision=HIGHEST,
                           preferred_element_type=F32)
            offs_s[...] = offs[0:1]
            run_s[...] = jnp.zeros_like(run_s)
            cnt_ref[...] = jnp.broadcast_to(cnt, (8, LANES))

        before = _dot(ls_ref[...], multi.astype(BF16)) + run_s[...]
        pos = before + offs_s[...]
        gates = jnp.zeros((tr, LANES), F32)
        exps = [jnp.exp(t - tops[0]) for t in tops]
        den = sum(exps)
        ones = jnp.ones((8, LANES), F32)
        sub = lax.broadcasted_iota(I32, (8, tr), 0)
        slots = jnp.zeros((8, tr), F32)
        for k in range(TOP_K):
            sk = lax.dot_general(ones, jnp.where(onehots[k], pos, 0.0), (((1,), (1,)), ((), ())),
                                 precision=HIGHEST, preferred_element_type=F32)
            slots = jnp.where(sub == k, sk, slots)
            gates = jnp.where(lane == k, exps[k] / den, gates)
        slots_ref[...] = slots.astype(I32)
        gates_ref[...] = gates
        run_s[...] += colsum


def moe_route(logits):
    t = logits.shape[0]
    tr = ROUTE_TILE
    nt = t // tr
    idx = jnp.arange(tr)
    strict_lower = (idx[:, None] > idx[None, :]).astype(BF16)
    ie = jnp.arange(LANES)
    strict_upper = (ie[:, None] < ie[None, :]).astype(F32)
    return pl.pallas_call(
        functools.partial(_route_kernel, n_exp=N_EXPERTS, tm=MOE_TILE),
        grid=(2, nt),
        in_specs=[pl.BlockSpec((tr, LANES), lambda p, j: (j, 0)),
                  pl.BlockSpec((tr, tr), lambda p, j: (0, 0)),
                  pl.BlockSpec((LANES, LANES), lambda p, j: (0, 0))],
        out_specs=[pl.BlockSpec((8, tr), lambda p, j: (j * p, 0)),
                   pl.BlockSpec((tr, LANES), lambda p, j: (j * p, 0)),
                   pl.BlockSpec((8, LANES), lambda p, j: (0, 0))],
        out_shape=[jax.ShapeDtypeStruct((nt * 8, tr), I32),
                   jax.ShapeDtypeStruct((t, LANES), F32),
                   jax.ShapeDtypeStruct((8, LANES), F32)],
        scratch_shapes=[pltpu.VMEM((1, LANES), F32)] * 3,
        compiler_params=_cparams(2),
        name="moe_route",
    )(logits, strict_lower, strict_upper)


def _row_copy(src_ref, src_row, dst_ref, dst_row, sem):
    return pltpu.make_async_copy(src_ref.at[pl.ds(src_row, 1)], dst_ref.at[pl.ds(dst_row, 1)], sem)


def _dispatch_kernel(pad_start_ref, pad_len_ref, slots_ref, tok_ref, xs_ref, zrow, sem, *, n_exp):
    i = pl.program_id(0)
    td = tok_ref.shape[0]

    @pl.when(i == 0)
    def _():
        zrow[...] = jnp.zeros_like(zrow)
        for start in (True, False):
            def per_expert(e, c, start=start):
                def per_row(r, c2):
                    cp = _row_copy(zrow, 0, xs_ref, pad_start_ref[e] + r, sem)
                    cp.start() if start else cp.wait()
                    return c2
                return lax.fori_loop(0, pad_len_ref[e], per_row, c)
            lax.fori_loop(0, n_exp, per_expert, 0)

    base = (i % (ROUTE_TILE // td)) * td
    for start in (True, False):
        def per_token(t, c, start=start):
            for k in range(TOP_K):
                cp = _row_copy(tok_ref, t, xs_ref, slots_ref[k * ROUTE_TILE + base + t] if start else 0, sem)
                cp.start(priority=k % 2) if start else cp.wait()
            return c
        lax.fori_loop(0, td, per_token, 0, unroll=4)


def moe_dispatch(tok, slots_flat, pad_start, pad_len, n_slots):
    t, d = tok.shape
    td = ROW_TILE
    per = ROUTE_TILE // td
    return pl.pallas_call(
        functools.partial(_dispatch_kernel, n_exp=N_EXPERTS),
        grid_spec=pltpu.PrefetchScalarGridSpec(
            num_scalar_prefetch=2,
            grid=(t // td,),
            in_specs=[pl.BlockSpec((ROUTE_TILE * TOP_K,), lambda i, *_: (i // per,), memory_space=pltpu.SMEM),
                      pl.BlockSpec((td, d), lambda i, *_: (i, 0))],
            out_specs=pl.BlockSpec(memory_space=pl.ANY),
            scratch_shapes=[pltpu.VMEM((8, d), F32), pltpu.SemaphoreType.DMA(())]),
        out_shape=jax.ShapeDtypeStruct((n_slots, d), F32),
        compiler_params=_cparams(1),
        name="moe_dispatch",
    )(pad_start, pad_len, slots_flat, tok)


def _experts_kernel(be_ref, nu_ref, x_ref, w1_ref, b1_ref, w2_ref, b2_ref, o_ref, w1_s, w2_s):
    b = pl.program_id(0)
    prev = be_ref[jnp.maximum(b - 1, 0)]

    @pl.when(b < nu_ref[0])
    def _():
        @pl.when(jnp.logical_or(b == 0, be_ref[b] != prev))
        def _():
            w1_s[...] = w1_ref[0, 0].astype(BF16)
            w2_s[...] = w2_ref[0, 0].astype(BF16)

        u = _dot(x_ref[...].astype(BF16), w1_s[...]) + b1_ref[0, 0]
        f = u.shape[1] // 2
        glu = jnp.minimum(u[:, :f], SWIGLU_LIMIT)
        lin = jnp.clip(u[:, f:], -SWIGLU_LIMIT, SWIGLU_LIMIT)
        act = (glu * _sigmoid(SWIGLU_ALPHA * glu) * (lin + 1.0)).astype(BF16)
        o_ref[...] = _dot(act, w2_s[...]) + b2_ref[0, 0]


def moe_experts(xs, block_e, n_used, layer, w_in, b_in, w_out, b_out):
    n_slots, d = xs.shape
    depth, n_exp, _, f2 = w_in.shape
    tm = MOE_TILE
    blk = lambda b, be, nu: (jnp.minimum(b, nu[0] - 1), 0)
    exp = lambda b, be, nu: (layer, be[b], 0, 0)
    return pl.pallas_call(
        _experts_kernel,
        grid_spec=pltpu.PrefetchScalarGridSpec(
            num_scalar_prefetch=2,
            grid=(n_slots // tm,),
            in_specs=[pl.BlockSpec((tm, d), blk),
                      pl.BlockSpec((1, 1, d, f2), exp),
                      pl.BlockSpec((1, 1, 1, f2), exp),
                      pl.BlockSpec((1, 1, f2 // 2, d), exp),
                      pl.BlockSpec((1, 1, 1, d), exp)],
            out_specs=pl.BlockSpec((tm, d), blk),
            scratch_shapes=[pltpu.VMEM((d, f2), BF16), pltpu.VMEM((f2 // 2, d), BF16)]),
        out_shape=jax.ShapeDtypeStruct((n_slots, d), F32),
        compiler_params=_cparams(1, VMEM_LIMIT_MOE),
        name="moe_experts",
    )(block_e, n_used, xs, w_in, b_in.reshape(depth, n_exp, 1, f2), w_out, b_out.reshape(depth, n_exp, 1, d))


def _combine_kernel(s_cur_ref, s_nxt_ref, gates_ref, x_ref, g2_ref, lng_ref, lnb_ref, ys_ref, o_ref,
                    buf, sem, *, n_tiles):
    i = pl.program_id(0)
    tc = x_ref.shape[1]

    per = ROUTE_TILE // tc

    def gather(slot_ref, tile, s, start):
        base = (tile % per) * tc
        def per_token(t, c):
            for k in range(TOP_K):
                row = slot_ref[k * ROUTE_TILE + base + t] if start else 0
                cp = pltpu.make_async_copy(ys_ref.at[pl.ds(row, 1)], buf.at[s, k, pl.ds(t, 1)], sem.at[s])
                cp.start(priority=k % 2) if start else cp.wait()
            return c
        lax.fori_loop(0, tc, per_token, 0, unroll=4)

    @pl.when(i == 0)
    def _():
        gather(s_cur_ref, i, 0, True)

    @pl.when(i + 1 < n_tiles)
    def _():
        gather(s_nxt_ref, i + 1, (i + 1) % 2, True)

    s = i % 2
    gather(s_cur_ref, i, s, False)
    g = gates_ref[...]
    f = g[:, 0:1] * buf[s, 0]
    for k in range(1, TOP_K):
        f = f + g[:, k:k + 1] * buf[s, k]
    o_ref[0] = _layer_norm(DEEPNORM_ALPHA * x_ref[0] + g2_ref[0] * f, lng_ref[...], lnb_ref[...])


def moe_combine(ys, slots_flat, gates, xn, mods, lng, lnb, *, tiles_per_b, tile0, row_fn):
    bsz, _, d = xn.shape
    tc = ROW_TILE
    n_tiles = bsz * tiles_per_b
    where = lambda i: (i // tiles_per_b, i % tiles_per_b + tile0, 0)
    per = ROUTE_TILE // tc
    return pl.pallas_call(
        functools.partial(_combine_kernel, n_tiles=n_tiles),
        grid=(n_tiles,),
        in_specs=[pl.BlockSpec((ROUTE_TILE * TOP_K,), lambda i: (i // per,), memory_space=pltpu.SMEM),
                  pl.BlockSpec((ROUTE_TILE * TOP_K,), lambda i: (jnp.minimum(i + 1, n_tiles - 1) // per,),
                               memory_space=pltpu.SMEM),
                  pl.BlockSpec((tc, LANES), lambda i: (i, 0)),
                  pl.BlockSpec((1, tc, d), where),
                  pl.BlockSpec((1, 1, d), lambda i: (row_fn(i // tiles_per_b, i % tiles_per_b + tile0) * 6 + 5, 0, 0)),
                  pl.BlockSpec((1, d), lambda i: (0, 0)),
                  pl.BlockSpec((1, d), lambda i: (0, 0)),
                  pl.BlockSpec(memory_space=pl.ANY)],
        out_specs=pl.BlockSpec((1, tc, d), lambda i: (i // tiles_per_b, i % tiles_per_b, 0)),
        out_shape=jax.ShapeDtypeStruct((bsz, tiles_per_b * tc, d), F32),
        scratch_shapes=[pltpu.VMEM((2, TOP_K, tc, d), F32), pltpu.SemaphoreType.DMA((2,))],
        compiler_params=_cparams(1),
        name="moe_combine",
    )(slots_flat, slots_flat, gates, xn, mods, lng, lnb, ys)


def moe_ffn(tok, logits, layer, w_in, b_in, w_out, b_out):
    t = tok.shape[0]
    tm = MOE_TILE
    n_blocks = (t * TOP_K + N_EXPERTS * (tm - 1) + tm - 1) // tm
    slots, gates, cnt = moe_route(logits)
    counts = cnt[0, :N_EXPERTS].astype(I32)
    padded = (counts + tm - 1) // tm * tm
    padded_end = jnp.cumsum(padded)
    pad_start = padded_end - padded + counts
    n_used = (padded_end[-1:] // tm).astype(I32)
    block_e = jnp.minimum(jnp.sum((padded_end[None, :] <= (jnp.arange(n_blocks) * tm)[:, None]).astype(I32), axis=1),
                          N_EXPERTS - 1).astype(I32)
    slots_flat = slots.reshape(t // ROUTE_TILE, 8, ROUTE_TILE)[:, :TOP_K, :].reshape(-1)
    xs = moe_dispatch(tok, slots_flat, pad_start.astype(I32), (padded - counts).astype(I32), n_blocks * tm)
    ys = moe_experts(xs, block_e, n_used, layer, w_in, b_in, w_out, b_out)
    return ys, slots_flat, gates


def _mla_proj_kernel(*refs, with_q):
    if with_q:
        (x_ref, sc_ref, sh_ref, wa_ref, qn_ref, kvn_ref, wq_ref, wkv_ref, cos_ref, sin_ref,
         qn_out, qr_out, kv_out, kr_out) = refs
    else:
        x_ref, sc_ref, sh_ref, wa_ref, kvn_ref, wkv_ref, kv_out, kr_out = refs
    h = (x_ref[0] * (1.0 + sc_ref[0]) + sh_ref[0]).astype(BF16)
    a = _dot(h, wa_ref[...])
    q0, kv0 = 0, MLA_Q_LORA
    kr0 = MLA_Q_LORA + MLA_KV_LORA
    kp0 = kr0 + LANES
    kvn = _rms_norm(a[:, kv0:kv0 + MLA_KV_LORA], kvn_ref[...]).astype(BF16)
    kv_out[0] = _dot(kvn, wkv_ref[...]).astype(BF16)
    k_rope = a[:, kr0:kr0 + MLA_ROPE]
    if not with_q:
        kr_out[0] = k_rope.astype(BF16)
        return
    cos, sin = cos_ref[...], sin_ref[...]
    kr_out[0] = (k_rope * cos + a[:, kp0:kp0 + MLA_ROPE] * sin).astype(BF16)
    qn = _rms_norm(a[:, q0:q0 + MLA_Q_LORA], qn_ref[...]).astype(BF16)
    q = _dot(qn, wq_ref[...])
    n_nope = MLA_HEADS * MLA_NOPE
    qn_out[0] = (q[:, 0:n_nope] * MLA_SCALE).astype(BF16)
    for hd in range(MLA_HEADS):
        r0 = n_nope + hd * LANES
        p0 = n_nope + (MLA_HEADS + hd) * LANES
        qr = q[:, r0:r0 + MLA_ROPE] * cos + q[:, p0:p0 + MLA_ROPE] * sin
        qr_out[0, hd] = (qr * MLA_SCALE).astype(BF16)


def mla_proj(xs, mods, wa, q_norm, kv_norm, wq, wkv, cos, sin, *, tile0, n_rows, row_fn, with_q):
    bsz, _, d = xs.shape
    tm = ROW_TILE
    nt = n_rows // tm
    full = lambda shape: pl.BlockSpec(shape, lambda b, j: (0,) * len(shape))
    row = lambda b, j: row_fn(b, j + tile0)
    in_specs = [pl.BlockSpec((1, tm, d), lambda b, j: (b, j + tile0, 0)),
                _mk_mod_spec(d, 1, row), _mk_mod_spec(d, 0, row), full(wa.shape)]
    args = [xs, mods, mods, wa]
    kv_w = wkv.shape[1]
    kv_specs = [pl.BlockSpec((1, tm, kv_w), lambda b, j: (b, j, 0)),
                pl.BlockSpec((1, tm, MLA_ROPE), lambda b, j: (b, j, 0))]
    kv_shapes = [jax.ShapeDtypeStruct((bsz, n_rows, kv_w), BF16),
                 jax.ShapeDtypeStruct((bsz, n_rows, MLA_ROPE), BF16)]
    if with_q:
        in_specs += [full(q_norm.shape), full(kv_norm.shape), full(wq.shape), full(wkv.shape),
                     pl.BlockSpec((tm, MLA_ROPE), lambda b, j: (j, 0)),
                     pl.BlockSpec((tm, MLA_ROPE), lambda b, j: (j, 0))]
        args += [q_norm, kv_norm, wq, wkv, cos, sin]
        n_nope = MLA_HEADS * MLA_NOPE
        out_specs = [pl.BlockSpec((1, tm, n_nope), lambda b, j: (b, j, 0)),
                     pl.BlockSpec((1, MLA_HEADS, tm, MLA_ROPE), lambda b, j: (b, 0, j, 0))] + kv_specs
        out_shape = [jax.ShapeDtypeStruct((bsz, n_rows, n_nope), BF16),
                     jax.ShapeDtypeStruct((bsz, MLA_HEADS, n_rows, MLA_ROPE), BF16)] + kv_shapes
    else:
        in_specs += [full(kv_norm.shape), full(wkv.shape)]
        args += [kv_norm, wkv]
        out_specs, out_shape = kv_specs, kv_shapes
    return pl.pallas_call(
        functools.partial(_mla_proj_kernel, with_q=with_q),
        grid=(bsz, nt),
        in_specs=in_specs, out_specs=out_specs, out_shape=out_shape,
        compiler_params=_cparams(2),
        name="mla_proj_q" if with_q else "mla_proj_ctx",
    )(*args)


def _attn_kernel(qn_ref, qr_ref, knc_ref, knl_ref, vc_ref, vl_ref, krc_ref, krl_ref, o_ref, kcat, qcat,
                 *, n_ctx, kc):
    j = pl.program_id(2)
    n_lat = knl_ref.shape[1]
    rope_end = MLA_NOPE + MLA_ROPE

    @pl.when(j == 0)
    def _():
        kcat[0:n_ctx, 0:MLA_NOPE] = knc_ref[0]
        kcat[n_ctx:n_ctx + n_lat, 0:MLA_NOPE] = knl_ref[0]
        kcat[0:n_ctx, MLA_NOPE:rope_end] = krc_ref[0]
        kcat[n_ctx:n_ctx + n_lat, MLA_NOPE:rope_end] = krl_ref[0]
        kcat[:, rope_end:] = jnp.zeros((kcat.shape[0], kcat.shape[1] - rope_end), BF16)

    tq = qn_ref.shape[1]
    qcat[:, 0:MLA_NOPE] = qn_ref[0]
    qcat[:, MLA_NOPE:rope_end] = qr_ref[0, 0]
    qcat[:, rope_end:] = jnp.zeros((tq, qcat.shape[1] - rope_end), BF16)
    q = qcat[...]
    m = jnp.full((tq, 1), -jnp.inf, F32)
    l = jnp.zeros((tq, 1), F32)
    acc = jnp.zeros((tq, MLA_V), F32)
    chunks = [(0, n_ctx, vc_ref, 0)] + [(n_ctx + c0, kc, vl_ref, c0) for c0 in range(0, n_lat, kc)]
    for k0, kn, v_ref, v0 in chunks:
        s = _dot_nt(q, kcat[k0:k0 + kn])
        m_new = jnp.maximum(m, jnp.max(s, axis=-1, keepdims=True))
        alpha = jnp.exp(m - m_new)
        p = jnp.exp(s - m_new)
        l = alpha * l + jnp.sum(p, axis=-1, keepdims=True)
        acc = alpha * acc + _dot(p.astype(BF16), v_ref[0, v0:v0 + kn])
        m = m_new
    o_ref[0] = (acc / l).astype(BF16)


def mla_attention(qn, qr, kv_c, kv_l, kr_c, kr_l):
    bsz, n_lat, _ = qn.shape
    n_ctx = kv_c.shape[1]
    tq = ATTN_Q_TILE
    kc = min(ATTN_K_CHUNK, n_lat)
    cat_w = 2 * LANES
    return pl.pallas_call(
        functools.partial(_attn_kernel, n_ctx=n_ctx, kc=kc),
        grid=(bsz, MLA_HEADS, n_lat // tq),
        in_specs=[pl.BlockSpec((1, tq, MLA_NOPE), lambda b, h, j: (b, j, h)),
                  pl.BlockSpec((1, 1, tq, MLA_ROPE), lambda b, h, j: (b, h, j, 0)),
                  pl.BlockSpec((1, n_ctx, MLA_NOPE), lambda b, h, j: (b, 0, 2 * h)),
                  pl.BlockSpec((1, n_lat, MLA_NOPE), lambda b, h, j: (b, 0, 2 * h)),
                  pl.BlockSpec((1, n_ctx, MLA_V), lambda b, h, j: (b, 0, 2 * h + 1)),
                  pl.BlockSpec((1, n_lat, MLA_V), lambda b, h, j: (b, 0, 2 * h + 1)),
                  pl.BlockSpec((1, n_ctx, MLA_ROPE), lambda b, h, j: (b, 0, 0)),
                  pl.BlockSpec((1, n_lat, MLA_ROPE), lambda b, h, j: (b, 0, 0))],
        out_specs=pl.BlockSpec((1, tq, MLA_V), lambda b, h, j: (b, j, h)),
        out_shape=jax.ShapeDtypeStruct((bsz, n_lat, MLA_HEADS * MLA_V), BF16),
        scratch_shapes=[pltpu.VMEM((n_ctx + n_lat, cat_w), BF16), pltpu.VMEM((tq, cat_w), BF16)],
        compiler_params=_cparams(3),
        name="mla_attention",
    )(qn, qr, kv_c, kv_l, kv_c, kv_l, kr_c, kr_l)


def _mla_out_kernel(o_ref, w_ref, x_ref, g1_ref, sh2_ref, sc2_ref, lng_ref, lnb_ref, rw_ref, rb_ref,
                    xn_ref, tok_ref, lg_ref):
    _post_mixer(_dot(o_ref[0], w_ref[...]), x_ref, g1_ref, sh2_ref, sc2_ref, lng_ref, lnb_ref, rw_ref, rb_ref,
                xn_ref, tok_ref, lg_ref)


def mla_out(o, w_o, xs, mods, lng, lnb, rw, rb, *, tile0, row_fn):
    bsz, n_rows, dv = o.shape
    d = xs.shape[2]
    tm = ROW_TILE
    nt = n_rows // tm
    row = lambda b, j: row_fn(b, j + tile0)
    post_in, post_out = _post_specs(d, row, nt)
    return pl.pallas_call(
        _mla_out_kernel,
        grid=(bsz, nt),
        in_specs=[pl.BlockSpec((1, tm, dv), lambda b, j: (b, j, 0)),
                  pl.BlockSpec((dv, d), lambda b, j: (0, 0)),
                  pl.BlockSpec((1, tm, d), lambda b, j: (b, j + tile0, 0))] + post_in,
        out_specs=post_out,
        out_shape=[jax.ShapeDtypeStruct((bsz, n_rows, d), F32),
                   jax.ShapeDtypeStruct((bsz * n_rows, d), F32),
                   jax.ShapeDtypeStruct((bsz * n_rows, LANES), F32)],
        compiler_params=_cparams(2),
        name="mla_out",
    )(o, w_o, xs, mods, mods, mods, lng, lnb, rw, rb)


def _rope_tables(seq_len):
    t = jnp.arange(seq_len)
    row = (t // GRID_W).astype(F32)
    col = (t % GRID_W).astype(F32)
    inv = ROPE_BASE ** (-jnp.arange(ROPE_FREQS, dtype=F32) / ROPE_FREQS)
    ang = jnp.stack([row[:, None] * inv, col[:, None] * inv], axis=1)
    cos = jnp.broadcast_to(jnp.cos(ang)[:, :, None, :], (seq_len, 2, 2, ROPE_FREQS)).reshape(seq_len, MLA_ROPE)
    sign = jnp.array([-1.0, 1.0], F32)[None, None, :, None]
    sin = (jnp.sin(ang)[:, :, None, :] * sign).reshape(seq_len, MLA_ROPE)
    return cos, sin


def _rope_partner_perm():
    j = jnp.arange(MLA_ROPE)
    half = (j // ROPE_FREQS) % 2
    return jnp.where(half == 0, j + ROPE_FREQS, j - ROPE_FREQS)


def _pad_cols(w, n):
    return jnp.pad(w, ((0, 0), (0, n - w.shape[1])))


def _router_pad(rw, rb):
    return _pad_cols(rw, LANES), _pad_cols(rb[None, :], LANES)


def kernel(x, c, ctx, c_ctx, mod_w, mod_b, ln1_g, ln1_b, ln2_g, ln2_b, ssd_w_in, ssd_conv_w, ssd_conv_b, ssd_dt_bias, ssd_a_log, ssd_d, ssd_norm_w, ssd_w_out, mla_w_a, mla_q_norm, mla_kv_norm, mla_w_qb, mla_w_kvb, mla_w_o, router_w, router_b, moe_w_in, moe_b_in, moe_w_out, moe_b_out):
    bsz, n_lat, d = x.shape
    n_ctx = ctx.shape[1]
    assert mod_w.shape[0] == DEPTH == 2 and bsz + 1 <= MOD_ROWS
    assert n_ctx % ROW_TILE == 0 and n_lat % ROUTE_TILE == 0 and (bsz * n_ctx) % ROUTE_TILE == 0
    ctx_tiles = n_ctx // ROW_TILE
    lat_tiles = n_lat // ROW_TILE
    ctx_row = bsz
    row_fn = lambda b, j: jnp.where(j < ctx_tiles, ctx_row, b)

    cond = jnp.concatenate([c, c_ctx[None, :], jnp.zeros((MOD_ROWS - bsz - 1, d), F32)], axis=0)
    mods = adaln_mods(cond, mod_w, mod_b).reshape(DEPTH, MOD_ROWS * 6, 1, d)
    xs = jnp.concatenate([ctx, x], axis=1)

    n_heads = ssd_a_log.shape[2]
    d_inner = n_heads * SSD_HEAD_DIM
    conv_dim = ssd_conv_w.shape[2]
    w_in = ssd_w_in[0]
    w_zx = w_in[:, :d_inner + conv_dim].astype(BF16)
    w_dt = w_in[:, d_inner + conv_dim:]
    u, dt, dtT = ssd_in_proj(xs, mods[0], w_zx, _pad_cols(w_dt, LANES).astype(BF16), w_dt.T.astype(BF16),
                             _pad_cols(ssd_dt_bias[0].reshape(1, -1), LANES), ssd_dt_bias[0].reshape(-1, 1),
                             ctx_tiles=ctx_tiles, ctx_row=ctx_row, n_heads=n_heads)
    xbc = ssd_conv(u, jnp.pad(ssd_conv_w[0], ((0, 8 - SSD_CONV), (0, 0))), ssd_conv_b[0][None, :],
                   col0=d_inner, ctx_tiles=ctx_tiles)
    y = ssd_scan(xbc, dt, dtT, ssd_a_log[0], ctx_chunks=n_ctx // SSD_CHUNK, groups=SSD_GROUPS,
                 n_heads=n_heads, head_dim=SSD_HEAD_DIM, n_state=SSD_STATE)
    rw0, rb0 = _router_pad(router_w[0], router_b[0])
    xn, tok, logits = ssd_out(y, xbc, u, xs, mods[0], jnp.repeat(ssd_d[0], SSD_HEAD_DIM)[None, :],
                              ssd_norm_w[0][None, :], ssd_w_out[0].astype(BF16),
                              ln1_g[0][None, :], ln1_b[0][None, :], rw0, rb0, ctx_tiles=ctx_tiles, ctx_row=ctx_row)
    ys, slots, gates = moe_ffn(tok, logits, 0, moe_w_in, moe_b_in, moe_w_out, moe_b_out)
    xs = moe_combine(ys, slots, gates, xn, mods[0], ln2_g[0][None, :], ln2_b[0][None, :],
                     tiles_per_b=ctx_tiles + lat_tiles, tile0=0, row_fn=row_fn)

    perm = _rope_partner_perm()
    wa = mla_w_a[0]
    kr0 = MLA_Q_LORA + MLA_KV_LORA
    w_kr = wa[:, kr0:kr0 + MLA_ROPE]
    wa_ext = jnp.concatenate([wa[:, :kr0], _pad_cols(w_kr, LANES), _pad_cols(w_kr[:, perm], LANES)], axis=1).astype(BF16)
    wq = mla_w_qb[0].reshape(MLA_Q_LORA, MLA_HEADS, MLA_NOPE + MLA_ROPE)
    wq_rope = wq[:, :, MLA_NOPE:]
    pad_heads = lambda w: jnp.pad(w, ((0, 0), (0, 0), (0, LANES - MLA_ROPE))).reshape(MLA_Q_LORA, MLA_HEADS * LANES)
    wq_ext = jnp.concatenate([wq[:, :, :MLA_NOPE].reshape(MLA_Q_LORA, MLA_HEADS * MLA_NOPE),
                              pad_heads(wq_rope), pad_heads(wq_rope[:, :, perm])], axis=1).astype(BF16)
    wkv = mla_w_kvb[0].astype(BF16)
    cos, sin = _rope_tables(n_lat)
    q_norm, kv_norm = mla_q_norm[0][None, :], mla_kv_norm[0][None, :]
    qn, qr, kv_l, kr_l = mla_proj(xs, mods[1], wa_ext, q_norm, kv_norm, wq_ext, wkv, cos, sin,
                                  tile0=ctx_tiles, n_rows=n_lat, row_fn=row_fn, with_q=True)
    kv_c, kr_c = mla_proj(xs, mods[1], wa_ext, q_norm, kv_norm, wq_ext, wkv, cos, sin,
                          tile0=0, n_rows=n_ctx, row_fn=row_fn, with_q=False)
    o = mla_attention(qn, qr, kv_c, kv_l, kr_c, kr_l)
    rw1, rb1 = _router_pad(router_w[1], router_b[1])
    xn, tok, logits = mla_out(o, mla_w_o[0].astype(BF16), xs, mods[1], ln1_g[1][None, :], ln1_b[1][None, :],
                              rw1, rb1, tile0=ctx_tiles, row_fn=row_fn)
    ys, slots, gates = moe_ffn(tok, logits, 1, moe_w_in, moe_b_in, moe_w_out, moe_b_out)
    lat_row = lambda b, j: b
    return moe_combine(ys, slots, gates, xn, mods[1], ln2_g[1][None, :], ln2_b[1][None, :],
                       tiles_per_b=lat_tiles, tile0=0, row_fn=lat_row)
```

```python
import functools
import math

import jax
import jax.numpy as jnp
from jax import lax
from jax.experimental import pallas as pl
from jax.experimental.pallas import tpu as pltpu

F32, BF16, I32 = jnp.float32, jnp.bfloat16, jnp.int32
HIGHEST = lax.Precision.HIGHEST

DEPTH = 2
GRID_W = 64
DEEPNORM_ALPHA = (2 * DEPTH) ** 0.25
LN_EPS = 1e-5
RMS_EPS = 1e-6
GATED_NORM_EPS = 1e-5
SSD_HEAD_DIM = 64
SSD_GROUPS = 8
SSD_STATE = 128
SSD_CONV = 5
SSD_CHUNK = 128
MLA_HEADS = 8
MLA_NOPE = 128
MLA_ROPE = 64
MLA_V = 128
MLA_Q_LORA = 384
MLA_KV_LORA = 256
MLA_SCALE = (MLA_NOPE + MLA_ROPE) ** -0.5
ROPE_FREQS = MLA_ROPE // 4
ROPE_BASE = 10000.0
N_EXPERTS = 32
TOP_K = 4
SWIGLU_LIMIT = 7.0
SWIGLU_ALPHA = 1.702

LANES = 128
ROW_TILE = 256
MOE_TILE = 256
ROUTE_TILE = 512
ATTN_Q_TILE = 512
ATTN_K_CHUNK = 1024
HALO = 16
MOD_ROWS = 8
VMEM_LIMIT = 48 * 1024 * 1024
VMEM_LIMIT_MOE = 56 * 1024 * 1024


def _cparams(n_axes, vmem=VMEM_LIMIT):
    return pltpu.CompilerParams(dimension_semantics=("arbitrary",) * n_axes, vmem_limit_bytes=vmem)


def _sigmoid(v):
    return 1.0 / (1.0 + jnp.exp(-v))


def _softplus(v):
    return jnp.maximum(v, 0.0) + jnp.log1p(jnp.exp(-jnp.abs(v)))


def _layer_norm(v, g, b):
    mu = jnp.mean(v, axis=-1, keepdims=True)
    d = v - mu
    var = jnp.mean(d * d, axis=-1, keepdims=True)
    return d * lax.rsqrt(var + LN_EPS) * g + b


def _rms_norm(v, g):
    return v * lax.rsqrt(jnp.mean(v * v, axis=-1, keepdims=True) + RMS_EPS) * g


def _dot(a, b):
    return jnp.dot(a, b, preferred_element_type=F32)


def _dot_nt(a, b):
    return lax.dot_general(a, b, (((1,), (1,)), ((), ())), preferred_element_type=F32)


def _dot_tn(a, b):
    return lax.dot_general(a, b, (((0,), (0,)), ((), ())), preferred_element_type=F32)


def _split_bf16(v):
    hi = v.astype(BF16)
    lo = (v - hi.astype(F32)).astype(BF16)
    return hi, lo


def _mod_kernel(c_ref, w_ref, b_ref, o_ref):
    c = c_ref[...]
    s = (c * _sigmoid(c)).astype(BF16)
    o_ref[0] = _dot(s, w_ref[0].astype(BF16)) + b_ref[0]


def adaln_mods(cond, mod_w, mod_b):
    depth, d, n = mod_w.shape
    tn = n // 4
    return pl.pallas_call(
        _mod_kernel,
        grid=(depth, n // tn),
        in_specs=[pl.BlockSpec((MOD_ROWS, d), lambda l, j: (0, 0)),
                  pl.BlockSpec((1, d, tn), lambda l, j: (l, 0, j)),
                  pl.BlockSpec((1, 1, tn), lambda l, j: (l, 0, j))],
        out_specs=pl.BlockSpec((1, MOD_ROWS, tn), lambda l, j: (l, 0, j)),
        out_shape=jax.ShapeDtypeStruct((depth, MOD_ROWS, n), F32),
        compiler_params=_cparams(2),
        name="adaln_mods",
    )(cond, mod_w, mod_b.reshape(depth, 1, n))


def _mk_mod_spec(d, which, row_fn):
    return pl.BlockSpec((1, 1, d), lambda *idx: (row_fn(*idx) * 6 + which, 0, 0))


def _ssd_in_kernel(x_ref, sc_ref, sh_ref, w_ref, wdt_ref, wdtT_ref, bias_ref, biasT_ref,
                   u_ref, dt_ref, dtT_ref, *, n_heads):
    h = (x_ref[0] * (1.0 + sc_ref[0]) + sh_ref[0]).astype(BF16)
    n_out = u_ref.shape[2]
    step = 512
    for n in range(n_out // step):
        u_ref[0, :, n * step:(n + 1) * step] = _dot(h, w_ref[:, n * step:(n + 1) * step]).astype(BF16)
    dt = _softplus(_dot(h, wdt_ref[...]) + bias_ref[...])
    dt_ref[0, 0] = dt[:, 0:n_heads]
    dt_ref[0, 1] = dt[:, n_heads:2 * n_heads]
    dtT = _softplus(_dot_nt(wdtT_ref[...], h) + biasT_ref[...])
    dtT_ref[0, 0] = dtT[0:n_heads]
    dtT_ref[0, 1] = dtT[n_heads:2 * n_heads]


def ssd_in_proj(xs, mods, w_zx, w_dt, w_dtT, dt_bias, dt_biasT, *, ctx_tiles, ctx_row, n_heads):
    bsz, s, d = xs.shape
    n_out = w_zx.shape[1]
    tm = ROW_TILE
    row = lambda b, j: jnp.where(j < ctx_tiles, ctx_row, b)
    return pl.pallas_call(
        functools.partial(_ssd_in_kernel, n_heads=n_heads),
        grid=(bsz, s // tm),
        in_specs=[pl.BlockSpec((1, tm, d), lambda b, j: (b, j, 0)),
                  _mk_mod_spec(d, 1, row), _mk_mod_spec(d, 0, row),
                  pl.BlockSpec((d, n_out), lambda b, j: (0, 0)),
                  pl.BlockSpec((d, LANES), lambda b, j: (0, 0)),
                  pl.BlockSpec((2 * n_heads, d), lambda b, j: (0, 0)),
                  pl.BlockSpec((1, LANES), lambda b, j: (0, 0)),
                  pl.BlockSpec((2 * n_heads, 1), lambda b, j: (0, 0))],
        out_specs=[pl.BlockSpec((1, tm, n_out), lambda b, j: (b, j, 0)),
                   pl.BlockSpec((1, 2, tm, n_heads), lambda b, j: (b, 0, j, 0)),
                   pl.BlockSpec((1, 2, n_heads, tm), lambda b, j: (b, 0, 0, j))],
        out_shape=[jax.ShapeDtypeStruct((bsz, s, n_out), BF16),
                   jax.ShapeDtypeStruct((bsz, 2, s, n_heads), F32),
                   jax.ShapeDtypeStruct((bsz, 2, n_heads, s), F32)],
        compiler_params=_cparams(2),
        name="ssd_in_proj",
    )(xs, mods, mods, w_zx, w_dt, w_dtT, dt_bias, dt_biasT)


def _ssd_conv_kernel(prev_ref, cur_ref, next_ref, w_ref, b_ref, o_ref, scr, *, tc, ctx_tiles, n_tiles):
    j = pl.program_id(1)
    has_prev = jnp.logical_and(j != 0, j != ctx_tiles)
    has_next = jnp.logical_and(j != ctx_tiles - 1, j != n_tiles - 1)
    scr[0:HALO] = jnp.where(has_prev, prev_ref[0].astype(F32), 0.0)
    scr[HALO:HALO + tc] = cur_ref[0].astype(F32)
    scr[HALO + tc:2 * HALO + tc] = jnp.where(has_next, next_ref[0].astype(F32), 0.0)
    pad = SSD_CONV // 2
    acc = b_ref[...] + w_ref[0:1] * scr[HALO - pad:HALO - pad + tc]
    for k in range(1, SSD_CONV):
        acc = acc + w_ref[k:k + 1] * scr[HALO - pad + k:HALO - pad + k + tc]
    o_ref[0] = (acc * _sigmoid(acc)).astype(BF16)


def ssd_conv(u, conv_w, conv_b, *, col0, ctx_tiles):
    bsz, s, _ = u.shape
    c = conv_w.shape[1]
    tc, wc = ROW_TILE, 2048
    n_tiles = s // tc
    cb0 = col0 // wc
    hb = tc // HALO
    return pl.pallas_call(
        functools.partial(_ssd_conv_kernel, tc=tc, ctx_tiles=ctx_tiles, n_tiles=n_tiles),
        grid=(bsz, n_tiles, c // wc),
        in_specs=[pl.BlockSpec((1, HALO, wc), lambda b, j, k: (b, jnp.maximum(j * hb - 1, 0), cb0 + k)),
                  pl.BlockSpec((1, tc, wc), lambda b, j, k: (b, j, cb0 + k)),
                  pl.BlockSpec((1, HALO, wc), lambda b, j, k: (b, jnp.minimum((j + 1) * hb, s // HALO - 1), cb0 + k)),
                  pl.BlockSpec((8, wc), lambda b, j, k: (0, k)),
                  pl.BlockSpec((1, wc), lambda b, j, k: (0, k))],
        out_specs=pl.BlockSpec((1, tc, wc), lambda b, j, k: (b, j, k)),
        out_shape=jax.ShapeDtypeStruct((bsz, s, c), BF16),
        scratch_shapes=[pltpu.VMEM((tc + 2 * HALO, wc), F32)],
        compiler_params=_cparams(3),
        name="ssd_conv",
    )(u, u, u, conv_w, conv_b)


def _ssd_scan_kernel(x_ref, b_ref, c_ref, dt_ref, dtT_ref, alog_ref, alogT_ref, tri_ref, triT_ref, e_ref,
                     y_ref, st_ref, *, groups, heads_per_group, head_dim, n_state):
    step = pl.program_id(2)

    @pl.when(step == 0)
    def _():
        st_ref[...] = jnp.zeros_like(st_ref)

    L = x_ref.shape[1]
    n_heads = groups * heads_per_group
    gw = heads_per_group * head_dim
    dt = dt_ref[0, 0]
    dtT = dtT_ref[0, 0]
    la = dt * (-jnp.exp(alog_ref[0]))
    laT = dtT * (-jnp.exp(alogT_ref[0]))
    tri = tri_ref[0]
    cum = jnp.dot(tri, la, precision=HIGHEST, preferred_element_type=F32)
    cumT = jnp.dot(laT, triT_ref[0], precision=HIGHEST, preferred_element_type=F32)
    tot = jnp.sum(la, axis=0, keepdims=True)
    ecum = jnp.exp(cum)
    w_end = dt * jnp.exp(tot - cum)
    cdec = jnp.exp(tot)
    stacked = jnp.concatenate([w_end, ecum, jnp.broadcast_to(cdec, (8, n_heads))], axis=0)
    s_hi, s_lo = _split_bf16(stacked)
    expanded = _dot(s_hi, e_ref[...]) + _dot(s_lo, e_ref[...])
    w_x, ecum_x, cdec_x = expanded[0:L], expanded[L:2 * L], expanded[2 * L:2 * L + 1]
    mask = tri > 0.5
    lane = lax.broadcasted_iota(I32, (L, 2 * head_dim), 1)
    for g in range(groups):
        bg = b_ref[0, :, g * n_state:(g + 1) * n_state]
        cg = c_ref[0, :, g * n_state:(g + 1) * n_state]
        cb = _dot_nt(cg, bg)
        mats = []
        for r in range(heads_per_group):
            h = g * heads_per_group + r
            seg = cum[:, h:h + 1] - cumT[h:h + 1, :]
            dec = jnp.exp(jnp.where(mask, seg, 0.0))
            mats.append((jnp.where(mask, cb * dec, 0.0) * dtT[h:h + 1, :]).astype(BF16))
        pairs = []
        for q in range(heads_per_group // 2):
            c0 = g * gw + q * 2 * head_dim
            xp = x_ref[0, :, c0:c0 + 2 * head_dim]
            pairs.append(jnp.where(lane < head_dim, _dot(mats[2 * q], xp), _dot(mats[2 * q + 1], xp)))
        y_diag = jnp.concatenate(pairs, axis=1)
        cols = slice(g * gw, (g + 1) * gw)
        xg = x_ref[0, :, cols]
        s_in = st_ref[g]
        y_off = _dot(cg, s_in.astype(BF16)) * ecum_x[:, cols]
        xw = (xg.astype(F32) * w_x[:, cols]).astype(BF16)
        st_ref[g] = s_in * cdec_x[:, cols] + _dot_tn(bg, xw)
        y_ref[0, 0, :, cols] = y_diag + y_off


def ssd_scan(xbc, dt, dtT, a_log, *, ctx_chunks, groups, n_heads, head_dim, n_state):
    bsz, s, _ = xbc.shape
    L = SSD_CHUNK
    nc = s // L
    d_inner = n_heads * head_dim
    gn = groups * n_state
    idx = jnp.arange(L)
    tril = (idx[:, None] >= idx[None, :]).astype(F32)
    tri = jnp.stack([tril, tril.T])
    expand = (jnp.arange(n_heads)[:, None] == (jnp.arange(d_inner)[None, :] // head_dim)).astype(BF16)

    def chunk(d, t):
        bwd = jnp.where(t < ctx_chunks, ctx_chunks - 1 - t, nc - 1 - (t - ctx_chunks))
        return jnp.where(d == 0, t, bwd)

    xb, bb, cbk = 0, d_inner // gn, d_inner // gn + 1
    return pl.pallas_call(
        functools.partial(_ssd_scan_kernel, groups=groups, heads_per_group=n_heads // groups,
                          head_dim=head_dim, n_state=n_state),
        grid=(bsz, 2, nc),
        in_specs=[pl.BlockSpec((1, L, d_inner), lambda b, d, t: (b, chunk(d, t), xb)),
                  pl.BlockSpec((1, L, gn), lambda b, d, t: (b, chunk(d, t), bb)),
                  pl.BlockSpec((1, L, gn), lambda b, d, t: (b, chunk(d, t), cbk)),
                  pl.BlockSpec((1, 1, L, n_heads), lambda b, d, t: (b, d, chunk(d, t), 0)),
                  pl.BlockSpec((1, 1, n_heads, L), lambda b, d, t: (b, d, 0, chunk(d, t))),
                  pl.BlockSpec((1, 1, n_heads), lambda b, d, t: (d, 0, 0)),
                  pl.BlockSpec((1, n_heads, 1), lambda b, d, t: (d, 0, 0)),
                  pl.BlockSpec((1, L, L), lambda b, d, t: (d, 0, 0)),
                  pl.BlockSpec((1, L, L), lambda b, d, t: (1 - d, 0, 0)),
                  pl.BlockSpec((n_heads, d_inner), lambda b, d, t: (0, 0))],
        out_specs=pl.BlockSpec((1, 1, L, d_inner), lambda b, d, t: (d, b, chunk(d, t), 0)),
        out_shape=jax.ShapeDtypeStruct((2, bsz, s, d_inner), F32),
        scratch_shapes=[pltpu.VMEM((groups, n_state, d_inner // groups), F32)],
        compiler_params=_cparams(3),
        name="ssd_scan",
    )(xbc, xbc, xbc, dt, dtT, a_log.reshape(2, 1, n_heads), a_log.reshape(2, n_heads, 1), tri, tri, expand)


def _post_mixer(o, x_ref, g1_ref, sh2_ref, sc2_ref, lng_ref, lnb_ref, rw_ref, rb_ref, xn_ref, tok_ref, lg_ref):
    xn = _layer_norm(DEEPNORM_ALPHA * x_ref[0] + g1_ref[0] * o, lng_ref[...], lnb_ref[...])
    xn_ref[0] = xn
    tok = xn * (1.0 + sc2_ref[0]) + sh2_ref[0]
    tok_ref[...] = tok
    t_hi, t_lo = _split_bf16(tok)
    w_hi, w_lo = _split_bf16(rw_ref[...])
    lg_ref[...] = _dot(t_hi, w_hi) + _dot(t_lo, w_hi) + _dot(t_hi, w_lo) + rb_ref[...]


def _ssd_out_kernel(yf_ref, yb_ref, xa_ref, z_ref, dx_ref, nw_ref, w_ref,
                    x_ref, g1_ref, sh2_ref, sc2_ref, lng_ref, lnb_ref, rw_ref, rb_ref,
                    xn_ref, tok_ref, lg_ref, *, groups):
    z = z_ref[0].astype(F32)
    y = (yf_ref[0, 0] + yb_ref[0, 0] + xa_ref[0].astype(F32) * dx_ref[...]) * (z * _sigmoid(z))
    gw = y.shape[1] // groups
    parts = []
    for g in range(groups):
        yg = y[:, g * gw:(g + 1) * gw]
        parts.append(yg * lax.rsqrt(jnp.mean(yg * yg, axis=-1, keepdims=True) + GATED_NORM_EPS))
    yn = (jnp.concatenate(parts, axis=1) * nw_ref[...]).astype(BF16)
    _post_mixer(_dot(yn, w_ref[...]), x_ref, g1_ref, sh2_ref, sc2_ref, lng_ref, lnb_ref, rw_ref, rb_ref,
                xn_ref, tok_ref, lg_ref)


def _post_specs(d, row, nt):
    ins = [_mk_mod_spec(d, 2, row), _mk_mod_spec(d, 3, row), _mk_mod_spec(d, 4, row),
           pl.BlockSpec((1, d), lambda b, j: (0, 0)), pl.BlockSpec((1, d), lambda b, j: (0, 0)),
           pl.BlockSpec((d, LANES), lambda b, j: (0, 0)), pl.BlockSpec((1, LANES), lambda b, j: (0, 0))]
    outs = [pl.BlockSpec((1, ROW_TILE, d), lambda b, j: (b, j, 0)),
            pl.BlockSpec((ROW_TILE, d), lambda b, j: (b * nt + j, 0)),
            pl.BlockSpec((ROW_TILE, LANES), lambda b, j: (b * nt + j, 0))]
    return ins, outs


def ssd_out(y, xbc, u, xs, mods, d_x, norm_w, w_out, lng, lnb, rw, rb, *, ctx_tiles, ctx_row):
    bsz, s, d = xs.shape
    d_inner = y.shape[3]
    tm = ROW_TILE
    nt = s // tm
    row = lambda b, j: jnp.where(j < ctx_tiles, ctx_row, b)
    post_in, post_out = _post_specs(d, row, nt)
    return pl.pallas_call(
        functools.partial(_ssd_out_kernel, groups=SSD_GROUPS),
        grid=(bsz, nt),
        in_specs=[pl.BlockSpec((1, 1, tm, d_inner), lambda b, j: (0, b, j, 0)),
                  pl.BlockSpec((1, 1, tm, d_inner), lambda b, j: (1, b, j, 0)),
                  pl.BlockSpec((1, tm, d_inner), lambda b, j: (b, j, 0)),
                  pl.BlockSpec((1, tm, d_inner), lambda b, j: (b, j, 0)),
                  pl.BlockSpec((1, d_inner), lambda b, j: (0, 0)),
                  pl.BlockSpec((1, d_inner), lambda b, j: (0, 0)),
                  pl.BlockSpec((d_inner, d), lambda b, j: (0, 0)),
                  pl.BlockSpec((1, tm, d), lambda b, j: (b, j, 0))] + post_in,
        out_specs=post_out,
        out_shape=[jax.ShapeDtypeStruct((bsz, s, d), F32),
                   jax.ShapeDtypeStruct((bsz * s, d), F32),
                   jax.ShapeDtypeStruct((bsz * s, LANES), F32)],
        compiler_params=_cparams(2),
        name="ssd_out",
    )(y, y, xbc, u, d_x, norm_w, w_out, xs, mods, mods, mods, lng, lnb, rw, rb)


def _route_kernel(lg_ref, ls_ref, up_ref, slots_ref, gates_ref, cnt_ref, cnt_s, run_s, offs_s, *, n_exp, tm):
    p, j = pl.program_id(0), pl.program_id(1)
    tr = lg_ref.shape[0]
    lane = lax.broadcasted_iota(I32, (tr, LANES), 1).astype(F32)
    lg = jnp.where(lane < n_exp, lg_ref[...], -jnp.inf)
    onehots, tops = [], []
    for _ in range(TOP_K):
        m = jnp.max(lg, axis=-1, keepdims=True)
        idx = jnp.min(jnp.where(lg == m, lane, float(LANES)), axis=-1, keepdims=True)
        oh = lane == idx
        onehots.append(oh)
        tops.append(m)
        lg = jnp.where(oh, -jnp.inf, lg)
    multi = sum(jnp.where(oh, 1.0, 0.0) for oh in onehots)
    colsum = jnp.sum(multi, axis=0, keepdims=True)

    @pl.when(jnp.logical_and(p == 0, j == 0))
    def _():
        cnt_s[...] = jnp.zeros_like(cnt_s)

    @pl.when(p == 0)
    def _():
        cnt_s[...] += colsum

    @pl.when(p == 1)
    def _():
        @pl.when(j == 0)
        def _():
            cnt = cnt_s[...]
            padded = jnp.ceil(cnt * (1.0 / tm)) * tm
            offs = jnp.dot(jnp.broadcast_to(padded, (8, LANES)), up_ref[...], precision=HIGHEST,
                           preferred_element_type=F32)
            offs_s[...] = offs[0:1]
            run_s[...] = jnp.zeros_like(run_s)
            cnt_ref[...] = jnp.broadcast_to(cnt, (8, LANES))

        before = _dot(ls_ref[...], multi.astype(BF16)) + run_s[...]
        pos = before + offs_s[...]
        gates = jnp.zeros((tr, LANES), F32)
        exps = [jnp.exp(t - tops[0]) for t in tops]
        den = sum(exps)
        ones = jnp.ones((8, LANES), F32)
        sub = lax.broadcasted_iota(I32, (8, tr), 0)
        slots = jnp.zeros((8, tr), F32)
        for k in range(TOP_K):
            sk = lax.dot_general(ones, jnp.where(onehots[k], pos, 0.0), (((1,), (1,)), ((), ())),
                                 precision=HIGHEST, preferred_element_type=F32)
            slots = jnp.where(sub == k, sk, slots)
            gates = jnp.where(lane == k, exps[k] / den, gates)
        slots_ref[...] = slots.astype(I32)
        gates_ref[...] = gates
        run_s[...] += colsum


def moe_route(logits):
    t = logits.shape[0]
    tr = ROUTE_TILE
    nt = t // tr
    idx = jnp.arange(tr)
    strict_lower = (idx[:, None] > idx[None, :]).astype(BF16)
    ie = jnp.arange(LANES)
    strict_upper = (ie[:, None] < ie[None, :]).astype(F32)
    return pl.pallas_call(
        functools.partial(_route_kernel, n_exp=N_EXPERTS, tm=MOE_TILE),
        grid=(2, nt),
        in_specs=[pl.BlockSpec((tr, LANES), lambda p, j: (j, 0)),
                  pl.BlockSpec((tr, tr), lambda p, j: (0, 0)),
                  pl.BlockSpec((LANES, LANES), lambda p, j: (0, 0))],
        out_specs=[pl.BlockSpec((8, tr), lambda p, j: (j * p, 0)),
                   pl.BlockSpec((tr, LANES), lambda p, j: (j * p, 0)),
                   pl.BlockSpec((8, LANES), lambda p, j: (0, 0))],
        out_shape=[jax.ShapeDtypeStruct((nt * 8, tr), I32),
                   jax.ShapeDtypeStruct((t, LANES), F32),
                   jax.ShapeDtypeStruct((8, LANES), F32)],
        scratch_shapes=[pltpu.VMEM((1, LANES), F32)] * 3,
        compiler_params=_cparams(2),
        name="moe_route",
    )(logits, strict_lower, strict_upper)


def _row_copy(src_ref, src_row, dst_ref, dst_row, sem):
    return pltpu.make_async_copy(src_ref.at[pl.ds(src_row, 1)], dst_ref.at[pl.ds(dst_row, 1)], sem)


def _dispatch_kernel(pad_start_ref, pad_len_ref, slots_ref, tok_ref, xs_ref, zrow, sem, *, n_exp):
    i = pl.program_id(0)
    td = tok_ref.shape[0]

    @pl.when(i == 0)
    def _():
        zrow[...] = jnp.zeros_like(zrow)
        for start in (True, False):
            def per_expert(e, c, start=start):
                def per_row(r, c2):
                    cp = _row_copy(zrow, 0, xs_ref, pad_start_ref[e] + r, sem)
                    cp.start() if start else cp.wait()
                    return c2
                return lax.fori_loop(0, pad_len_ref[e], per_row, c)
            lax.fori_loop(0, n_exp, per_expert, 0)

    base = (i % (ROUTE_TILE // td)) * td
    for start in (True, False):
        def per_token(t, c, start=start):
            for k in range(TOP_K):
                cp = _row_copy(tok_ref, t, xs_ref, slots_ref[k * ROUTE_TILE + base + t] if start else 0, sem)
                cp.start(priority=k % 2) if start else cp.wait()
            return c
        lax.fori_loop(0, td, per_token, 0, unroll=4)


def moe_dispatch(tok, slots_flat, pad_start, pad_len, n_slots):
    t, d = tok.shape
    td = ROW_TILE
    per = ROUTE_TILE // td
    return pl.pallas_call(
        functools.partial(_dispatch_kernel, n_exp=N_EXPERTS),
        grid_spec=pltpu.PrefetchScalarGridSpec(
            num_scalar_prefetch=2,
            grid=(t // td,),
            in_specs=[pl.BlockSpec((ROUTE_TILE * TOP_K,), lambda i, *_: (i // per,), memory_space=pltpu.SMEM),
                      pl.BlockSpec((td, d), lambda i, *_: (i, 0))],
            out_specs=pl.BlockSpec(memory_space=pl.ANY),
            scratch_shapes=[pltpu.VMEM((8, d), F32), pltpu.SemaphoreType.DMA(())]),
        out_shape=jax.ShapeDtypeStruct((n_slots, d), F32),
        compiler_params=_cparams(1),
        name="moe_dispatch",
    )(pad_start, pad_len, slots_flat, tok)


def _experts_kernel(be_ref, nu_ref, x_ref, w1_ref, b1_ref, w2_ref, b2_ref, o_ref, w1_s, w2_s):
    b = pl.program_id(0)
    prev = be_ref[jnp.maximum(b - 1, 0)]

    @pl.when(b < nu_ref[0])
    def _():
        @pl.when(jnp.logical_or(b == 0, be_ref[b] != prev))
        def _():
            w1_s[...] = w1_ref[0, 0].astype(BF16)
            w2_s[...] = w2_ref[0, 0].astype(BF16)

        u = _dot(x_ref[...].astype(BF16), w1_s[...]) + b1_ref[0, 0]
        f = u.shape[1] // 2
        glu = jnp.minimum(u[:, :f], SWIGLU_LIMIT)
        lin = jnp.clip(u[:, f:], -SWIGLU_LIMIT, SWIGLU_LIMIT)
        act = (glu * _sigmoid(SWIGLU_ALPHA * glu) * (lin + 1.0)).astype(BF16)
        o_ref[...] = _dot(act, w2_s[...]) + b2_ref[0, 0]


def moe_experts(xs, block_e, n_used, layer, w_in, b_in, w_out, b_out):
    n_slots, d = xs.shape
    depth, n_exp, _, f2 = w_in.shape
    tm = MOE_TILE
    blk = lambda b, be, nu: (jnp.minimum(b, nu[0] - 1), 0)
    exp = lambda b, be, nu: (layer, be[b], 0, 0)
    return pl.pallas_call(
        _experts_kernel,
        grid_spec=pltpu.PrefetchScalarGridSpec(
            num_scalar_prefetch=2,
            grid=(n_slots // tm,),
            in_specs=[pl.BlockSpec((tm, d), blk),
                      pl.BlockSpec((1, 1, d, f2), exp),
                      pl.BlockSpec((1, 1, 1, f2), exp),
                      pl.BlockSpec((1, 1, f2 // 2, d), exp),
                      pl.BlockSpec((1, 1, 1, d), exp)],
            out_specs=pl.BlockSpec((tm, d), blk),
            scratch_shapes=[pltpu.VMEM((d, f2), BF16), pltpu.VMEM((f2 // 2, d), BF16)]),
        out_shape=jax.ShapeDtypeStruct((n_slots, d), F32),
        compiler_params=_cparams(1, VMEM_LIMIT_MOE),
        name="moe_experts",
    )(block_e, n_used, xs, w_in, b_in.reshape(depth, n_exp, 1, f2), w_out, b_out.reshape(depth, n_exp, 1, d))


def _combine_kernel(s_cur_ref, s_nxt_ref, gates_ref, x_ref, g2_ref, lng_ref, lnb_ref, ys_ref, o_ref,
                    buf, sem, *, n_tiles):
    i = pl.program_id(0)
    tc = x_ref.shape[1]

    per = ROUTE_TILE // tc

    def gather(slot_ref, tile, s, start):
        base = (tile % per) * tc
        def per_token(t, c):
            for k in range(TOP_K):
                row = slot_ref[k * ROUTE_TILE + base + t] if start else 0
                cp = pltpu.make_async_copy(ys_ref.at[pl.ds(row, 1)], buf.at[s, k, pl.ds(t, 1)], sem.at[s])
                cp.start(priority=k % 2) if start else cp.wait()
            return c
        lax.fori_loop(0, tc, per_token, 0, unroll=4)

    @pl.when(i == 0)
    def _():
        gather(s_cur_ref, i, 0, True)

    @pl.when(i + 1 < n_tiles)
    def _():
        gather(s_nxt_ref, i + 1, (i + 1) % 2, True)

    s = i % 2
    gather(s_cur_ref, i, s, False)
    g = gates_ref[...]
    f = g[:, 0:1] * buf[s, 0]
    for k in range(1, TOP_K):
        f = f + g[:, k:k + 1] * buf[s, k]
    o_ref[0] = _layer_norm(DEEPNORM_ALPHA * x_ref[0] + g2_ref[0] * f, lng_ref[...], lnb_ref[...])


def moe_combine(ys, slots_flat, gates, xn, mods, lng, lnb, *, tiles_per_b, tile0, row_fn):
    bsz, _, d = xn.shape
    tc = ROW_TILE
    n_tiles = bsz * tiles_per_b
    where = lambda i: (i // tiles_per_b, i % tiles_per_b + tile0, 0)
    per = ROUTE_TILE // tc
    return pl.pallas_call(
        functools.partial(_combine_kernel, n_tiles=n_tiles),
        grid=(n_tiles,),
        in_specs=[pl.BlockSpec((ROUTE_TILE * TOP_K,), lambda i: (i // per,), memory_space=pltpu.SMEM),
                  pl.BlockSpec((ROUTE_TILE * TOP_K,), lambda i: (jnp.minimum(i + 1, n_tiles - 1) // per,),
                               memory_space=pltpu.SMEM),
                  pl.BlockSpec((tc, LANES), lambda i: (i, 0)),
                  pl.BlockSpec((1, tc, d), where),
                  pl.BlockSpec((1, 1, d), lambda i: (row_fn(i // tiles_per_b, i % tiles_per_b + tile0) * 6 + 5, 0, 0)),
                  pl.BlockSpec((1, d), lambda i: (0, 0)),
                  pl.BlockSpec((1, d), lambda i: (0, 0)),
                  pl.BlockSpec(memory_space=pl.ANY)],
        out_specs=pl.BlockSpec((1, tc, d), lambda i: (i // tiles_per_b, i % tiles_per_b, 0)),
        out_shape=jax.ShapeDtypeStruct((bsz, tiles_per_b * tc, d), F32),
        scratch_shapes=[pltpu.VMEM((2, TOP_K, tc, d), F32), pltpu.SemaphoreType.DMA((2,))],
        compiler_params=_cparams(1),
        name="moe_combine",
    )(slots_flat, slots_flat, gates, xn, mods, lng, lnb, ys)


def moe_ffn(tok, logits, layer, w_in, b_in, w_out, b_out):
    t = tok.shape[0]
    tm = MOE_TILE
    n_blocks = (t * TOP_K + N_EXPERTS * (tm - 1) + tm - 1) // tm
    slots, gates, cnt = moe_route(logits)
    counts = cnt[0, :N_EXPERTS].astype(I32)
    padded = (counts + tm - 1) // tm * tm
    padded_end = jnp.cumsum(padded)
    pad_start = padded_end - padded + counts
    n_used = (padded_end[-1:] // tm).astype(I32)
    block_e = jnp.minimum(jnp.sum((padded_end[None, :] <= (jnp.arange(n_blocks) * tm)[:, None]).astype(I32), axis=1),
                          N_EXPERTS - 1).astype(I32)
    slots_flat = slots.reshape(t // ROUTE_TILE, 8, ROUTE_TILE)[:, :TOP_K, :].reshape(-1)
    xs = moe_dispatch(tok, slots_flat, pad_start.astype(I32), (padded - counts).astype(I32), n_blocks * tm)
    ys = moe_experts(xs, block_e, n_used, layer, w_in, b_in, w_out, b_out)
    return ys, slots_flat, gates


def _mla_proj_kernel(*refs, with_q):
    if with_q:
        (x_ref, sc_ref, sh_ref, wa_ref, qn_ref, kvn_ref, wq_ref, wkv_ref, cos_ref, sin_ref,
         qn_out, qr_out, kv_out, kr_out) = refs
    else:
        x_ref, sc_ref, sh_ref, wa_ref, kvn_ref, wkv_ref, kv_out, kr_out = refs
    h = (x_ref[0] * (1.0 + sc_ref[0]) + sh_ref[0]).astype(BF16)
    a = _dot(h, wa_ref[...])
    q0, kv0 = 0, MLA_Q_LORA
    kr0 = MLA_Q_LORA + MLA_KV_LORA
    kp0 = kr0 + LANES
    kvn = _rms_norm(a[:, kv0:kv0 + MLA_KV_LORA], kvn_ref[...]).astype(BF16)
    kv_out[0] = _dot(kvn, wkv_ref[...]).astype(BF16)
    k_rope = a[:, kr0:kr0 + MLA_ROPE]
    if not with_q:
        kr_out[0] = k_rope.astype(BF16)
        return
    cos, sin = cos_ref[...], sin_ref[...]
    kr_out[0] = (k_rope * cos + a[:, kp0:kp0 + MLA_ROPE] * sin).astype(BF16)
    qn = _rms_norm(a[:, q0:q0 + MLA_Q_LORA], qn_ref[...]).astype(BF16)
    q = _dot(qn, wq_ref[...])
    n_nope = MLA_HEADS * MLA_NOPE
    qn_out[0] = (q[:, 0:n_nope] * MLA_SCALE).astype(BF16)
    for hd in range(MLA_HEADS):
        r0 = n_nope + hd * LANES
        p0 = n_nope + (MLA_HEADS + hd) * LANES
        qr = q[:, r0:r0 + MLA_ROPE] * cos + q[:, p0:p0 + MLA_ROPE] * sin
        qr_out[0, hd] = (qr * MLA_SCALE).astype(BF16)


def mla_proj(xs, mods, wa, q_norm, kv_norm, wq, wkv, cos, sin, *, tile0, n_rows, row_fn, with_q):
    bsz, _, d = xs.shape
    tm = ROW_TILE
    nt = n_rows // tm
    full = lambda shape: pl.BlockSpec(shape, lambda b, j: (0,) * len(shape))
    row = lambda b, j: row_fn(b, j + tile0)
    in_specs = [pl.BlockSpec((1, tm, d), lambda b, j: (b, j + tile0, 0)),
                _mk_mod_spec(d, 1, row), _mk_mod_spec(d, 0, row), full(wa.shape)]
    args = [xs, mods, mods, wa]
    kv_w = wkv.shape[1]
    kv_specs = [pl.BlockSpec((1, tm, kv_w), lambda b, j: (b, j, 0)),
                pl.BlockSpec((1, tm, MLA_ROPE), lambda b, j: (b, j, 0))]
    kv_shapes = [jax.ShapeDtypeStruct((bsz, n_rows, kv_w), BF16),
                 jax.ShapeDtypeStruct((bsz, n_rows, MLA_ROPE), BF16)]
    if with_q:
        in_specs += [full(q_norm.shape), full(kv_norm.shape), full(wq.shape), full(wkv.shape),
                     pl.BlockSpec((tm, MLA_ROPE), lambda b, j: (j, 0)),
                     pl.BlockSpec((tm, MLA_ROPE), lambda b, j: (j, 0))]
        args += [q_norm, kv_norm, wq, wkv, cos, sin]
        n_nope = MLA_HEADS * MLA_NOPE
        out_specs = [pl.BlockSpec((1, tm, n_nope), lambda b, j: (b, j, 0)),
                     pl.BlockSpec((1, MLA_HEADS, tm, MLA_ROPE), lambda b, j: (b, 0, j, 0))] + kv_specs
        out_shape = [jax.ShapeDtypeStruct((bsz, n_rows, n_nope), BF16),
                     jax.ShapeDtypeStruct((bsz, MLA_HEADS, n_rows, MLA_ROPE), BF16)] + kv_shapes
    else:
        in_specs += [full(kv_norm.shape), full(wkv.shape)]
        args += [kv_norm, wkv]
        out_specs, out_shape = kv_specs, kv_shapes
    return pl.pallas_call(
        functools.partial(_mla_proj_kernel, with_q=with_q),
        grid=(bsz, nt),
        in_specs=in_specs, out_specs=out_specs, out_shape=out_shape,
        compiler_params=_cparams(2),
        name="mla_proj_q" if with_q else "mla_proj_ctx",
    )(*args)


def _attn_kernel(qn_ref, qr_ref, knc_ref, knl_ref, vc_ref, vl_ref, krc_ref, krl_ref, o_ref, kcat, qcat,
                 *, n_ctx, kc):
    j = pl.program_id(2)
    n_lat = knl_ref.shape[1]
    rope_end = MLA_NOPE + MLA_ROPE

    @pl.when(j == 0)
    def _():
        kcat[0:n_ctx, 0:MLA_NOPE] = knc_ref[0]
        kcat[n_ctx:n_ctx + n_lat, 0:MLA_NOPE] = knl_ref[0]
        kcat[0:n_ctx, MLA_NOPE:rope_end] = krc_ref[0]
        kcat[n_ctx:n_ctx + n_lat, MLA_NOPE:rope_end] = krl_ref[0]
        kcat[:, rope_end:] = jnp.zeros((kcat.shape[0], kcat.shape[1] - rope_end), BF16)

    tq = qn_ref.shape[1]
    qcat[:, 0:MLA_NOPE] = qn_ref[0]
    qcat[:, MLA_NOPE:rope_end] = qr_ref[0, 0]
    qcat[:, rope_end:] = jnp.zeros((tq, qcat.shape[1] - rope_end), BF16)
    q = qcat[...]
    m = jnp.full((tq, 1), -jnp.inf, F32)
    l = jnp.zeros((tq, 1), F32)
    acc = jnp.zeros((tq, MLA_V), F32)
    chunks = [(0, n_ctx, vc_ref, 0)] + [(n_ctx + c0, kc, vl_ref, c0) for c0 in range(0, n_lat, kc)]
    for k0, kn, v_ref, v0 in chunks:
        s = _dot_nt(q, kcat[k0:k0 + kn])
        m_new = jnp.maximum(m, jnp.max(s, axis=-1, keepdims=True))
        alpha = jnp.exp(m - m_new)
        p = jnp.exp(s - m_new)
        l = alpha * l + jnp.sum(p, axis=-1, keepdims=True)
        acc = alpha * acc + _dot(p.astype(BF16), v_ref[0, v0:v0 + kn])
        m = m_new
    o_ref[0] = (acc / l).astype(BF16)


def mla_attention(qn, qr, kv_c, kv_l, kr_c, kr_l):
    bsz, n_lat, _ = qn.shape
    n_ctx = kv_c.shape[1]
    tq = ATTN_Q_TILE
    kc = min(ATTN_K_CHUNK, n_lat)
    cat_w = 2 * LANES
    return pl.pallas_call(
        functools.partial(_attn_kernel, n_ctx=n_ctx, kc=kc),
        grid=(bsz, MLA_HEADS, n_lat // tq),
        in_specs=[pl.BlockSpec((1, tq, MLA_NOPE), lambda b, h, j: (b, j, h)),
                  pl.BlockSpec((1, 1, tq, MLA_ROPE), lambda b, h, j: (b, h, j, 0)),
                  pl.BlockSpec((1, n_ctx, MLA_NOPE), lambda b, h, j: (b, 0, 2 * h)),
                  pl.BlockSpec((1, n_lat, MLA_NOPE), lambda b, h, j: (b, 0, 2 * h)),
                  pl.BlockSpec((1, n_ctx, MLA_V), lambda b, h, j: (b, 0, 2 * h + 1)),
                  pl.BlockSpec((1, n_lat, MLA_V), lambda b, h, j: (b, 0, 2 * h + 1)),
                  pl.BlockSpec((1, n_ctx, MLA_ROPE), lambda b, h, j: (b, 0, 0)),
                  pl.BlockSpec((1, n_lat, MLA_ROPE), lambda b, h, j: (b, 0, 0))],
        out_specs=pl.BlockSpec((1, tq, MLA_V), lambda b, h, j: (b, j, h)),
        out_shape=jax.ShapeDtypeStruct((bsz, n_lat, MLA_HEADS * MLA_V), BF16),
        scratch_shapes=[pltpu.VMEM((n_ctx + n_lat, cat_w), BF16), pltpu.VMEM((tq, cat_w), BF16)],
        compiler_params=_cparams(3),
        name="mla_attention",
    )(qn, qr, kv_c, kv_l, kv_c, kv_l, kr_c, kr_l)


def _mla_out_kernel(o_ref, w_ref, x_ref, g1_ref, sh2_ref, sc2_ref, lng_ref, lnb_ref, rw_ref, rb_ref,
                    xn_ref, tok_ref, lg_ref):
    _post_mixer(_dot(o_ref[0], w_ref[...]), x_ref, g1_ref, sh2_ref, sc2_ref, lng_ref, lnb_ref, rw_ref, rb_ref,
                xn_ref, tok_ref, lg_ref)


def mla_out(o, w_o, xs, mods, lng, lnb, rw, rb, *, tile0, row_fn):
    bsz, n_rows, dv = o.shape
    d = xs.shape[2]
    tm = ROW_TILE
    nt = n_rows // tm
    row = lambda b, j: row_fn(b, j + tile0)
    post_in, post_out = _post_specs(d, row, nt)
    return pl.pallas_call(
        _mla_out_kernel,
        grid=(bsz, nt),
        in_specs=[pl.BlockSpec((1, tm, dv), lambda b, j: (b, j, 0)),
                  pl.BlockSpec((dv, d), lambda b, j: (0, 0)),
                  pl.BlockSpec((1, tm, d), lambda b, j: (b, j + tile0, 0))] + post_in,
        out_specs=post_out,
        out_shape=[jax.ShapeDtypeStruct((bsz, n_rows, d), F32),
                   jax.ShapeDtypeStruct((bsz * n_rows, d), F32),
                   jax.ShapeDtypeStruct((bsz * n_rows, LANES), F32)],
        compiler_params=_cparams(2),
        name="mla_out",
    )(o, w_o, xs, mods, mods, mods, lng, lnb, rw, rb)


def _rope_tables(seq_len):
    t = jnp.arange(seq_len)
    row = (t // GRID_W).astype(F32)
    col = (t % GRID_W).astype(F32)
    inv = ROPE_BASE ** (-jnp.arange(ROPE_FREQS, dtype=F32) / ROPE_FREQS)
    ang = jnp.stack([row[:, None] * inv, col[:, None] * inv], axis=1)
    cos = jnp.broadcast_to(jnp.cos(ang)[:, :, None, :], (seq_len, 2, 2, ROPE_FREQS)).reshape(seq_len, MLA_ROPE)
    sign = jnp.array([-1.0, 1.0], F32)[None, None, :, None]
    sin = (jnp.sin(ang)[:, :, None, :] * sign).reshape(seq_len, MLA_ROPE)
    return cos, sin


def _rope_partner_perm():
    j = jnp.arange(MLA_ROPE)
    half = (j // ROPE_FREQS) % 2
    return jnp.where(half == 0, j + ROPE_FREQS, j - ROPE_FREQS)


def _pad_cols(w, n):
    return jnp.pad(w, ((0, 0), (0, n - w.shape[1])))


def _router_pad(rw, rb):
    return _pad_cols(rw, LANES), _pad_cols(rb[None, :], LANES)


def kernel(x, c, ctx, c_ctx, mod_w, mod_b, ln1_g, ln1_b, ln2_g, ln2_b, ssd_w_in, ssd_conv_w, ssd_conv_b, ssd_dt_bias, ssd_a_log, ssd_d, ssd_norm_w, ssd_w_out, mla_w_a, mla_q_norm, mla_kv_norm, mla_w_qb, mla_w_kvb, mla_w_o, router_w, router_b, moe_w_in, moe_b_in, moe_w_out, moe_b_out):
    bsz, n_lat, d = x.shape
    n_ctx = ctx.shape[1]
    assert mod_w.shape[0] == DEPTH == 2 and bsz + 1 <= MOD_ROWS
    assert n_ctx % ROW_TILE == 0 and n_lat % ROUTE_TILE == 0 and (bsz * n_ctx) % ROUTE_TILE == 0
    ctx_tiles = n_ctx // ROW_TILE
    lat_tiles = n_lat // ROW_TILE
    ctx_row = bsz
    row_fn = lambda b, j: jnp.where(j < ctx_tiles, ctx_row, b)

    cond = jnp.concatenate([c, c_ctx[None, :], jnp.zeros((MOD_ROWS - bsz - 1, d), F32)], axis=0)
    mods = adaln_mods(cond, mod_w, mod_b).reshape(DEPTH, MOD_ROWS * 6, 1, d)
    xs = jnp.concatenate([ctx, x], axis=1)

    n_heads = ssd_a_log.shape[2]
    d_inner = n_heads * SSD_HEAD_DIM
    conv_dim = ssd_conv_w.shape[2]
    w_in = ssd_w_in[0]
    w_zx = w_in[:, :d_inner + conv_dim].astype(BF16)
    w_dt = w_in[:, d_inner + conv_dim:]
    u, dt, dtT = ssd_in_proj(xs, mods[0], w_zx, _pad_cols(w_dt, LANES).astype(BF16), w_dt.T.astype(BF16),
                             _pad_cols(ssd_dt_bias[0].reshape(1, -1), LANES), ssd_dt_bias[0].reshape(-1, 1),
                             ctx_tiles=ctx_tiles, ctx_row=ctx_row, n_heads=n_heads)
    xbc = ssd_conv(u, jnp.pad(ssd_conv_w[0], ((0, 8 - SSD_CONV), (0, 0))), ssd_conv_b[0][None, :],
                   col0=d_inner, ctx_tiles=ctx_tiles)
    y = ssd_scan(xbc, dt, dtT, ssd_a_log[0], ctx_chunks=n_ctx // SSD_CHUNK, groups=SSD_GROUPS,
                 n_heads=n_heads, head_dim=SSD_HEAD_DIM, n_state=SSD_STATE)
    rw0, rb0 = _router_pad(router_w[0], router_b[0])
    xn, tok, logits = ssd_out(y, xbc, u, xs, mods[0], jnp.repeat(ssd_d[0], SSD_HEAD_DIM)[None, :],
                              ssd_norm_w[0][None, :], ssd_w_out[0].astype(BF16),
                              ln1_g[0][None, :], ln1_b[0][None, :], rw0, rb0, ctx_tiles=ctx_tiles, ctx_row=ctx_row)
    ys, slots, gates = moe_ffn(tok, logits, 0, moe_w_in, moe_b_in, moe_w_out, moe_b_out)
    xs = moe_combine(ys, slots, gates, xn, mods[0], ln2_g[0][None, :], ln2_b[0][None, :],
                     tiles_per_b=ctx_tiles + lat_tiles, tile0=0, row_fn=row_fn)

    perm = _rope_partner_perm()
    wa = mla_w_a[0]
    kr0 = MLA_Q_LORA + MLA_KV_LORA
    w_kr = wa[:, kr0:kr0 + MLA_ROPE]
    wa_ext = jnp.concatenate([wa[:, :kr0], _pad_cols(w_kr, LANES), _pad_cols(w_kr[:, perm], LANES)], axis=1).astype(BF16)
    wq = mla_w_qb[0].reshape(MLA_Q_LORA, MLA_HEADS, MLA_NOPE + MLA_ROPE)
    wq_rope = wq[:, :, MLA_NOPE:]
    pad_heads = lambda w: jnp.pad(w, ((0, 0), (0, 0), (0, LANES - MLA_ROPE))).reshape(MLA_Q_LORA, MLA_HEADS * LANES)
    wq_ext = jnp.concatenate([wq[:, :, :MLA_NOPE].reshape(MLA_Q_LORA, MLA_HEADS * MLA_NOPE),
                              pad_heads(wq_rope), pad_heads(wq_rope[:, :, perm])], axis=1).astype(BF16)
    wkv = mla_w_kvb[0].astype(BF16)
    cos, sin = _rope_tables(n_lat)
    q_norm, kv_norm = mla_q_norm[0][None, :], mla_kv_norm[0][None, :]
    qn, qr, kv_l, kr_l = mla_proj(xs, mods[1], wa_ext, q_norm, kv_norm, wq_ext, wkv, cos, sin,
                                  tile0=ctx_tiles, n_rows=n_lat, row_fn=row_fn, with_q=True)
    kv_c, kr_c = mla_proj(xs, mods[1], wa_ext, q_norm, kv_norm, wq_ext, wkv, cos, sin,
                          tile0=0, n_rows=n_ctx, row_fn=row_fn, with_q=False)
    o = mla_attention(qn, qr, kv_c, kv_l, kr_c, kr_l)
    rw1, rb1 = _router_pad(router_w[1], router_b[1])
    xn, tok, logits = mla_out(o, mla_w_o[0].astype(BF16), xs, mods[1], ln1_g[1][None, :], ln1_b[1][None, :],
                              rw1, rb1, tile0=ctx_tiles, row_fn=row_fn)
    ys, slots, gates = moe_ffn(tok, logits, 1, moe_w_in, moe_b_in, moe_w_out, moe_b_out)
    lat_row = lambda b, j: b
    return moe_combine(ys, slots, gates, xn, mods[1], ln2_g[1][None, :], ln2_b[1][None, :],
                       tiles_per_b=lat_tiles, tile0=0, row_fn=lat_row)
```
